```python
import math
import jax, jax.numpy as jnp
from jax import lax
import numpy as np

D_MODEL = 4096
BATCH = 8
SEQ = 2048
DEPTH = 1

FOURIER_WIDTH = D_MODEL // 2
FOURIER_GROUPS = 8
FOURIER_GROUP_DIM = FOURIER_WIDTH // FOURIER_GROUPS
DIFF_HEAD_DIM = 128
DIFF_HEADS = (D_MODEL // 2) // (2 * DIFF_HEAD_DIM)
ATTN_WIDTH = DIFF_HEADS * 2 * DIFF_HEAD_DIM
IN_COLS = FOURIER_WIDTH + 3 * ATTN_WIDTH + 2 * D_MODEL
SPLITS = (FOURIER_WIDTH,
          FOURIER_WIDTH + ATTN_WIDTH,
          FOURIER_WIDTH + 2 * ATTN_WIDTH,
          FOURIER_WIDTH + 3 * ATTN_WIDTH,
          FOURIER_WIDTH + 3 * ATTN_WIDTH + D_MODEL)
FFN_HIDDEN = -(-8 * D_MODEL // (3 * 256)) * 256
ROPE_THETA = 10000.0
RMS_EPS = 1e-6
Q_BLOCK = 128
LAMBDA_STD = 0.1

kernel_name = "hybrid_fourier_diffattn_gated_block"


def rmsnorm(x, g):
    xf = x.astype(jnp.float32)
    y = xf * lax.rsqrt(jnp.mean(xf * xf, axis=-1, keepdims=True) + RMS_EPS)
    return (y * g.astype(jnp.float32)).astype(x.dtype)


def rope_tables(positions):
    inv_freq = ROPE_THETA ** (-jnp.arange(0, DIFF_HEAD_DIM, 2, dtype=jnp.float32) / DIFF_HEAD_DIM)
    ang = positions.astype(jnp.float32)[..., None] * inv_freq
    return jnp.cos(ang)[:, :, None, None, :], jnp.sin(ang)[:, :, None, None, :]


def apply_rope(x, cos, sin):
    xf = x.astype(jnp.float32)
    x1, x2 = jnp.split(xf, 2, axis=-1)
    out = jnp.concatenate([x1 * cos - x2 * sin, x2 * cos + x1 * sin], axis=-1)
    return out.astype(x.dtype)


def fourier_mix(f):
    b, s, _ = f.shape
    fg = f.reshape(b, s, FOURIER_GROUPS, FOURIER_GROUP_DIM).astype(jnp.float32)
    y = jnp.fft.fft2(fg, axes=(1, 3), norm="ortho").real
    return y.reshape(b, s, FOURIER_WIDTH).astype(f.dtype)


def diff_attention(q, k, v, lam):
    b, s = q.shape[0], q.shape[1]
    nblk = s // Q_BLOCK
    scale = 1.0 / math.sqrt(DIFF_HEAD_DIM)
    qb = q.reshape(b, nblk, Q_BLOCK, DIFF_HEADS, 2, DIFF_HEAD_DIM).transpose(1, 0, 3, 4, 2, 5)
    kt = k.transpose(0, 2, 3, 1, 4)
    vt = v.transpose(0, 2, 1, 3)

    def attend_block(qblk):
        sc = jnp.einsum('bhcqd,bhckd->bhcqk', qblk.astype(jnp.float32), kt.astype(jnp.float32)) * scale
        p = jax.nn.softmax(sc, axis=-1)
        a = (p[:, :, 0] - lam * p[:, :, 1]).astype(vt.dtype)
        return jnp.einsum('bhqk,bhkv->bhqv', a, vt)

    o = lax.map(attend_block, qb)
    return o.transpose(1, 0, 3, 2, 4).reshape(b, s, DIFF_HEADS, 2 * DIFF_HEAD_DIM)


def setup_inputs(seed: int = 0) -> dict:
    key = jax.random.key(seed)
    ks = jax.random.split(key, 20)
    f32 = jnp.float32

    def nrm(k, shape, fan_in):
        return jax.random.normal(k, shape, f32) * (fan_in ** -0.5)

    def gain(k, shape):
        return 1.0 + 0.02 * jax.random.normal(k, shape, f32)

    x = jax.random.normal(ks[0], (BATCH, SEQ, D_MODEL), f32)
    offsets = jax.random.randint(ks[1], (BATCH, 1), 0, 64, dtype=jnp.int32)
    positions = jnp.arange(SEQ, dtype=jnp.int32)[None, :] + offsets
    return {
        "x": x,
        "positions": positions,
        "norm_mix_g": gain(ks[2], (DEPTH, D_MODEL)),
        "w_in": nrm(ks[3], (DEPTH, D_MODEL, IN_COLS), D_MODEL),
        "b_gate": 0.01 * jax.random.normal(ks[4], (DEPTH, 2, D_MODEL), f32),
        "lambda_q1": LAMBDA_STD * jax.random.normal(ks[5], (DEPTH, DIFF_HEAD_DIM), f32),
        "lambda_k1": LAMBDA_STD * jax.random.normal(ks[6], (DEPTH, DIFF_HEAD_DIM), f32),
        "lambda_q2": LAMBDA_STD * jax.random.normal(ks[7], (DEPTH, DIFF_HEAD_DIM), f32),
        "lambda_k2": LAMBDA_STD * jax.random.normal(ks[8], (DEPTH, DIFF_HEAD_DIM), f32),
        "subln_g": gain(ks[9], (DEPTH, 2 * DIFF_HEAD_DIM)),
        "w_fourier_out": nrm(ks[10], (DEPTH, FOURIER_WIDTH, D_MODEL), FOURIER_WIDTH),
        "w_attn_out": nrm(ks[11], (DEPTH, ATTN_WIDTH, D_MODEL), ATTN_WIDTH),
        "w_out": nrm(ks[12], (DEPTH, D_MODEL, D_MODEL), D_MODEL),
        "norm_ffn_g": gain(ks[13], (DEPTH, D_MODEL)),
        "w_ffn_gate": nrm(ks[14], (DEPTH, D_MODEL, FFN_HIDDEN), D_MODEL),
        "w_ffn_up": nrm(ks[15], (DEPTH, D_MODEL, FFN_HIDDEN), D_MODEL),
        "w_ffn_down": nrm(ks[16], (DEPTH, FFN_HIDDEN, D_MODEL), FFN_HIDDEN),
        "norm_final_g": gain(ks[17], (D_MODEL,)),
    }


def reference(x, positions, norm_mix_g, w_in, b_gate, lambda_q1, lambda_k1, lambda_q2, lambda_k2,
              subln_g, w_fourier_out, w_attn_out, w_out, norm_ffn_g, w_ffn_gate, w_ffn_up,
              w_ffn_down, norm_final_g):
    b, s, _ = x.shape
    cos, sin = rope_tables(positions)
    h = x
    for l in range(DEPTH):
        u = rmsnorm(h, norm_mix_g[l])
        z = jnp.einsum('bsd,dc->bsc', u, w_in[l])
        f, q, k, v, gf, ga = jnp.split(z, SPLITS, axis=-1)
        gate_f = jax.nn.sigmoid(gf + b_gate[l, 0])
        gate_a = jax.nn.sigmoid(ga + b_gate[l, 1])

        y_f = jnp.einsum('bsf,fd->bsd', fourier_mix(f), w_fourier_out[l])

        q = apply_rope(q.reshape(b, s, DIFF_HEADS, 2, DIFF_HEAD_DIM), cos, sin)
        k = apply_rope(k.reshape(b, s, DIFF_HEADS, 2, DIFF_HEAD_DIM), cos, sin)
        v = v.reshape(b, s, DIFF_HEADS, 2 * DIFF_HEAD_DIM)
        lam_init = 0.8 - 0.6 * math.exp(-0.3 * l)
        lam = (jnp.exp(jnp.sum(lambda_q1[l].astype(jnp.float32) * lambda_k1[l].astype(jnp.float32)))
               - jnp.exp(jnp.sum(lambda_q2[l].astype(jnp.float32) * lambda_k2[l].astype(jnp.float32)))
               + lam_init)
        o = diff_attention(q, k, v, lam)
        o = rmsnorm(o, subln_g[l]) * (1.0 - lam_init)
        y_a = jnp.einsum('bsa,ad->bsd', o.reshape(b, s, ATTN_WIDTH), w_attn_out[l])

        merged = gate_f * y_f + gate_a * y_a
        h = h + jnp.einsum('bsd,de->bse', merged, w_out[l])

        u2 = rmsnorm(h, norm_ffn_g[l])
        hid = jax.nn.silu(jnp.einsum('bsd,df->bsf', u2, w_ffn_gate[l])) * jnp.einsum('bsd,df->bsf', u2, w_ffn_up[l])
        h = h + jnp.einsum('bsf,fd->bsd', hid, w_ffn_down[l])
    return rmsnorm(h, norm_final_g)
```

```python
import functools
import math

import numpy as np
import jax
import jax.numpy as jnp
from jax import lax
from jax.experimental import pallas as pl
from jax.experimental.pallas import tpu as pltpu

F32 = jnp.float32
BF16 = jnp.bfloat16

D_MODEL = 4096
FOURIER_WIDTH = D_MODEL // 2
FOURIER_GROUPS = 8
FOURIER_GROUP_DIM = FOURIER_WIDTH // FOURIER_GROUPS
HEAD_DIM = 128
HEADS = (D_MODEL // 2) // (2 * HEAD_DIM)
ATTN_WIDTH = HEADS * 2 * HEAD_DIM
FFN_HIDDEN = -(-8 * D_MODEL // (3 * 256)) * 256
ROPE_THETA = 10000.0
RMS_EPS = 1e-6

LANES = 128
VMEM_LIMIT = 56 * 1024 * 1024


def _params(sem):
    return pltpu.CompilerParams(dimension_semantics=sem, vmem_limit_bytes=VMEM_LIMIT)


def _rope_table_kernel(pos_ref, invf_ref, cos_ref, sin_ref):
    ang = pos_ref[...].astype(F32) * invf_ref[...]
    lane = lax.broadcasted_iota(jnp.int32, ang.shape, 1)
    s = jnp.sin(ang)
    cos_ref[...] = jnp.cos(ang)
    sin_ref[...] = jnp.where(lane < HEAD_DIM // 2, -s, s)


def _rope_tables(pos_col, invf_full, tm=2048):
    t = pos_col.shape[0]
    return pl.pallas_call(
        _rope_table_kernel,
        out_shape=(jax.ShapeDtypeStruct((t, HEAD_DIM), F32),
                   jax.ShapeDtypeStruct((t, HEAD_DIM), F32)),
        grid=(t // tm,),
        in_specs=[pl.BlockSpec((tm, 1), lambda i: (i, 0)),
                  pl.BlockSpec((1, HEAD_DIM), lambda i: (0, 0))],
        out_specs=(pl.BlockSpec((tm, HEAD_DIM), lambda i: (i, 0)),
                   pl.BlockSpec((tm, HEAD_DIM), lambda i: (i, 0))),
        compiler_params=_params(("arbitrary",)),
        name="rope_tables",
    )(pos_col, invf_full)


def _rmsnorm_kernel(x_ref, g_ref, o_ref):
    x = x_ref[...].astype(F32)
    ms = jnp.mean(x * x, axis=-1, keepdims=True)
    o_ref[...] = (x * lax.rsqrt(ms + RMS_EPS) * g_ref[...]).astype(o_ref.dtype)


def _rmsnorm(x, g_row, out_dtype, tm=256):
    t, d = x.shape
    return pl.pallas_call(
        _rmsnorm_kernel,
        out_shape=jax.ShapeDtypeStruct((t, d), out_dtype),
        grid=(t // tm,),
        in_specs=[pl.BlockSpec((tm, d), lambda i: (i, 0)),
                  pl.BlockSpec((1, d), lambda i: (0, 0))],
        out_specs=pl.BlockSpec((tm, d), lambda i: (i, 0)),
        compiler_params=_params(("arbitrary",)),
        name="rmsnorm",
    )(x, g_row)


def _mm_cast_kernel(a_ref, w_ref, o_ref):
    o_ref[...] = jnp.dot(a_ref[...], w_ref[...], preferred_element_type=F32).astype(o_ref.dtype)


def _mm_rope_kernel(a_ref, w_ref, cos_ref, sin_ref, cs_ref, o_ref):
    acc = jnp.dot(a_ref[...], w_ref[...], preferred_element_type=F32)
    cos = cos_ref[...]
    sin = sin_ref[...]
    for c in range(acc.shape[1] // HEAD_DIM):
        sl = slice(c * HEAD_DIM, (c + 1) * HEAD_DIM)
        xc = acc[:, sl]
        r = xc * cos + pltpu.roll(xc, HEAD_DIM // 2, 1) * sin
        o_ref[:, sl] = (r * cs_ref[:, sl]).astype(o_ref.dtype)


def _mm_sigmoid_kernel(a_ref, w_ref, b_ref, o_ref):
    acc = jnp.dot(a_ref[...], w_ref[...], preferred_element_type=F32)
    o_ref[...] = jax.nn.sigmoid(acc + b_ref[...]).astype(o_ref.dtype)


def _in_proj(u, w, col0, ncols, kind, extras=(), bm=1024, bn=512):
    t, k = u.shape
    joff = col0 // bn
    in_specs = [pl.BlockSpec((bm, k), lambda i, j: (i, 0)),
                pl.BlockSpec((k, bn), lambda i, j: (0, j + joff))]
    if kind == "cast":
        body = _mm_cast_kernel
    elif kind == "rope":
        body = _mm_rope_kernel
        in_specs += [pl.BlockSpec((bm, HEAD_DIM), lambda i, j: (i, 0)),
                     pl.BlockSpec((bm, HEAD_DIM), lambda i, j: (i, 0)),
                     pl.BlockSpec((1, bn), lambda i, j: (0, j))]
    else:
        body = _mm_sigmoid_kernel
        in_specs += [pl.BlockSpec((1, bn), lambda i, j: (0, j))]
    return pl.pallas_call(
        body,
        out_shape=jax.ShapeDtypeStruct((t, ncols), BF16),
        grid=(t // bm, ncols // bn),
        in_specs=in_specs,
        out_specs=pl.BlockSpec((bm, bn), lambda i, j: (i, j)),
        compiler_params=_params(("arbitrary", "arbitrary")),
        name="in_proj_" + kind,
    )(u, w, *extras)


def _dft_constants(seq):
    gd = FOURIER_GROUP_DIM
    jc = np.arange(gd, dtype=np.int64)
    ang_c = 2.0 * np.pi * ((jc[:, None] * jc[None, :]) % gd) / gd
    chan = np.concatenate([np.cos(ang_c), np.sin(ang_c)], axis=1) / math.sqrt(gd)
    js = np.arange(seq, dtype=np.int64)
    ang_s = 2.0 * np.pi * ((js[:, None] * js[None, :]) % seq) / seq
    seqm = np.concatenate([np.cos(ang_s), -np.sin(ang_s)], axis=1) / math.sqrt(seq)
    return jnp.asarray(chan, dtype=BF16), jnp.asarray(seqm, dtype=BF16)


def _fourier_kernel(f_ref, chan_ref, seqm_ref, o_ref, pq_ref):
    s = f_ref.shape[0]
    gd = FOURIER_GROUP_DIM
    pq = jnp.dot(f_ref[...], chan_ref[...], preferred_element_type=F32)
    pq_ref[0:s, :] = pq[:, 0:gd].astype(BF16)
    pq_ref[s:2 * s, :] = pq[:, gd:2 * gd].astype(BF16)
    o_ref[...] = jnp.dot(seqm_ref[...], pq_ref[...], preferred_element_type=F32).astype(o_ref.dtype)


def _fourier_mix(f, batch, seq, chan, seqm):
    t, width = f.shape
    gd = FOURIER_GROUP_DIM
    return pl.pallas_call(
        _fourier_kernel,
        out_shape=jax.ShapeDtypeStruct((t, width), BF16),
        grid=(batch, width // gd),
        in_specs=[pl.BlockSpec((seq, gd), lambda b, g: (b, g)),
                  pl.BlockSpec((gd, 2 * gd), lambda b, g: (0, 0)),
                  pl.BlockSpec((seq, 2 * seq), lambda b, g: (0, 0),
                               pipeline_mode=pl.Buffered(1))],
        out_specs=pl.BlockSpec((seq, gd), lambda b, g: (b, g)),
        scratch_shapes=[pltpu.VMEM((2 * seq, gd), BF16)],
        compiler_params=_params(("arbitrary", "arbitrary")),
        name="fourier_mix",
    )(f, chan, seqm)


def _attn_kernel(lam_ref, g_ref, q_ref, k_ref, v_ref, o_ref, *, lam_init):
    lp = lam_ref[...]
    lam = (jnp.exp(jnp.sum(lp[0:1] * lp[1:2], axis=-1, keepdims=True))
           - jnp.exp(jnp.sum(lp[2:3] * lp[3:4], axis=-1, keepdims=True)) + lam_init)
    q = q_ref[...]
    k = k_ref[...]
    v = v_ref[...]

    def component(c):
        sl = slice(c * HEAD_DIM, (c + 1) * HEAD_DIM)
        s = lax.dot_general(q[:, sl], k[:, sl], (((1,), (1,)), ((), ())),
                            preferred_element_type=F32)
        m = jnp.max(s, axis=-1, keepdims=True)
        p = jnp.exp(s - m)
        l = jnp.sum(p, axis=-1, keepdims=True)
        o = jnp.dot(p.astype(BF16), v, preferred_element_type=F32)
        return o / l

    o = component(0) - lam * component(1)
    ms = jnp.mean(o * o, axis=-1, keepdims=True)
    o = o * lax.rsqrt(ms + RMS_EPS) * g_ref[...] * (1.0 - lam_init)
    o_ref[...] = o.astype(o_ref.dtype)


def _diff_attention(qk, v, lam_params, sub_g, batch, seq, lam_init, tq=256):
    t = qk.shape[0]
    hw = 2 * HEAD_DIM
    nq = seq // tq
    return pl.pallas_call(
        functools.partial(_attn_kernel, lam_init=lam_init),
        out_shape=jax.ShapeDtypeStruct((t, ATTN_WIDTH), BF16),
        grid=(batch, HEADS, nq),
        in_specs=[pl.BlockSpec((4, HEAD_DIM), lambda b, h, i: (0, 0)),
                  pl.BlockSpec((1, hw), lambda b, h, i: (0, 0)),
                  pl.BlockSpec((tq, hw), lambda b, h, i: (b * nq + i, h)),
                  pl.BlockSpec((seq, hw), lambda b, h, i: (b, HEADS + h)),
                  pl.BlockSpec((seq, hw), lambda b, h, i: (b, h))],
        out_specs=pl.BlockSpec((tq, hw), lambda b, h, i: (b * nq + i, h)),
        compiler_params=_params(("arbitrary", "arbitrary", "arbitrary")),
        name="diff_attention",
    )(lam_params, sub_g, qk, qk, v)


def _merge_kernel(yf_ref, o_ref_in, wf_ref, wa_ref, gf_ref, ga_ref, out_ref):
    y_f = jnp.dot(yf_ref[...], wf_ref[...], preferred_element_type=F32)
    y_a = jnp.dot(o_ref_in[...], wa_ref[...], preferred_element_type=F32)
    out_ref[...] = (gf_ref[...].astype(F32) * y_f + ga_ref[...].astype(F32) * y_a).astype(out_ref.dtype)


def _merge(yf, oa, wf, wa, gates, bm=1024, bn=512):
    t, kf = yf.shape
    ka = oa.shape[1]
    d = wf.shape[1]
    ga_off = d // bn
    return pl.pallas_call(
        _merge_kernel,
        out_shape=jax.ShapeDtypeStruct((t, d), BF16),
        grid=(t // bm, d // bn),
        in_specs=[pl.BlockSpec((bm, kf), lambda i, j: (i, 0)),
                  pl.BlockSpec((bm, ka), lambda i, j: (i, 0)),
                  pl.BlockSpec((kf, bn), lambda i, j: (0, j)),
                  pl.BlockSpec((ka, bn), lambda i, j: (0, j)),
                  pl.BlockSpec((bm, bn), lambda i, j: (i, j)),
                  pl.BlockSpec((bm, bn), lambda i, j: (i, j + ga_off))],
        out_specs=pl.BlockSpec((bm, bn), lambda i, j: (i, j)),
        compiler_params=_params(("arbitrary", "arbitrary")),
        name="merge_proj",
    )(yf, oa, wf, wa, gates, gates)


def _mm_residual_kernel(a_ref, w_ref, r_ref, o_ref):
    acc = jnp.dot(a_ref[...], w_ref[...], preferred_element_type=F32)
    o_ref[...] = r_ref[...] + acc


def _mm_residual(a, w, res, bm=512, bn=512):
    t, k = a.shape
    n = w.shape[1]
    return pl.pallas_call(
        _mm_residual_kernel,
        out_shape=jax.ShapeDtypeStruct((t, n), F32),
        grid=(t // bm, n // bn),
        in_specs=[pl.BlockSpec((bm, k), lambda i, j: (i, 0)),
                  pl.BlockSpec((k, bn), lambda i, j: (0, j)),
                  pl.BlockSpec((bm, bn), lambda i, j: (i, j))],
        out_specs=pl.BlockSpec((bm, bn), lambda i, j: (i, j)),
        compiler_params=_params(("arbitrary", "arbitrary")),
        name="mm_residual",
    )(a, w, res)


def _swiglu_kernel(u_ref, wg_ref, wu_ref, o_ref):
    u = u_ref[...]
    gate = jnp.dot(u, wg_ref[...], preferred_element_type=F32)
    up = jnp.dot(u, wu_ref[...], preferred_element_type=F32)
    o_ref[...] = (jax.nn.silu(gate) * up).astype(o_ref.dtype)


def _swiglu(u, wg, wu, bm=1024, bn=256):
    t, k = u.shape
    n = wg.shape[1]
    return pl.pallas_call(
        _swiglu_kernel,
        out_shape=jax.ShapeDtypeStruct((t, n), BF16),
        grid=(t // bm, n // bn),
        in_specs=[pl.BlockSpec((bm, k), lambda i, j: (i, 0)),
                  pl.BlockSpec((k, bn), lambda i, j: (0, j)),
                  pl.BlockSpec((k, bn), lambda i, j: (0, j))],
        out_specs=pl.BlockSpec((bm, bn), lambda i, j: (i, j)),
        compiler_params=_params(("arbitrary", "arbitrary")),
        name="swiglu_up",
    )(u, wg, wu)


def kernel(x, positions, norm_mix_g, w_in, b_gate, lambda_q1, lambda_k1, lambda_q2, lambda_k2,
           subln_g, w_fourier_out, w_attn_out, w_out, norm_ffn_g, w_ffn_gate, w_ffn_up,
           w_ffn_down, norm_final_g):
    batch, seq, d = x.shape
    depth = w_in.shape[0]
    t = batch * seq
    h = x.reshape(t, d)

    inv_freq = ROPE_THETA ** (-jnp.arange(0, HEAD_DIM, 2, dtype=F32) / HEAD_DIM)
    invf_full = jnp.concatenate([inv_freq, inv_freq]).reshape(1, HEAD_DIM)
    cos_t, sin_t = _rope_tables(positions.reshape(t, 1), invf_full)
    qk_scale = jnp.concatenate([jnp.full((1, ATTN_WIDTH), 1.0 / math.sqrt(HEAD_DIM), F32),
                                jnp.ones((1, ATTN_WIDTH), F32)], axis=1)
    chan, seqm = _dft_constants(seq)

    for l in range(depth):
        lam_init = 0.8 - 0.6 * math.exp(-0.3 * l)
        w_in_b = w_in[l].astype(BF16)
        u = _rmsnorm(h, norm_mix_g[l].reshape(1, d), BF16)
        f = _in_proj(u, w_in_b, 0, FOURIER_WIDTH, "cast")
        qk = _in_proj(u, w_in_b, FOURIER_WIDTH, 2 * ATTN_WIDTH, "rope", (cos_t, sin_t, qk_scale))
        v = _in_proj(u, w_in_b, FOURIER_WIDTH + 2 * ATTN_WIDTH, ATTN_WIDTH, "cast")
        gates = _in_proj(u, w_in_b, FOURIER_WIDTH + 3 * ATTN_WIDTH, 2 * d, "sigmoid",
                         (b_gate[l].reshape(1, 2 * d),))

        y_f = _fourier_mix(f, batch, seq, chan, seqm)
        lam_params = jnp.stack([lambda_q1[l], lambda_k1[l], lambda_q2[l], lambda_k2[l]]).astype(F32)
        o_a = _diff_attention(qk, v, lam_params, subln_g[l].reshape(1, 2 * HEAD_DIM).astype(F32),
                              batch, seq, lam_init)

        merged = _merge(y_f, o_a, w_fourier_out[l].astype(BF16), w_attn_out[l].astype(BF16), gates)
        h = _mm_residual(merged, w_out[l].astype(BF16), h, bm=1024)

        u2 = _rmsnorm(h, norm_ffn_g[l].reshape(1, d), BF16)
        hid = _swiglu(u2, w_ffn_gate[l].astype(BF16), w_ffn_up[l].astype(BF16))
        h = _mm_residual(hid, w_ffn_down[l].astype(BF16), h)

    out = _rmsnorm(h, norm_final_g.reshape(1, d), x.dtype)
    return out.reshape(batch, seq, d)
```

```python
import functools
import math

import numpy as np
import jax
import jax.numpy as jnp
from jax import lax
from jax.experimental import pallas as pl
from jax.experimental.pallas import tpu as pltpu

F32 = jnp.float32
BF16 = jnp.bfloat16

D_MODEL = 4096
FOURIER_WIDTH = D_MODEL // 2
FOURIER_GROUPS = 8
FOURIER_GROUP_DIM = FOURIER_WIDTH // FOURIER_GROUPS
HEAD_DIM = 128
HEADS = (D_MODEL // 2) // (2 * HEAD_DIM)
ATTN_WIDTH = HEADS * 2 * HEAD_DIM
FFN_HIDDEN = -(-8 * D_MODEL // (3 * 256)) * 256
ROPE_THETA = 10000.0
RMS_EPS = 1e-6

LANES = 128
BF16_SUBLANES = 16
VMEM_LIMIT = 56 * 1024 * 1024


def _params(sem, flags=None):
    return pltpu.CompilerParams(dimension_semantics=sem, vmem_limit_bytes=VMEM_LIMIT, flags=flags)


def _rope_table_kernel(pos_ref, invf_ref, cos_ref, sin_ref):
    ang = pos_ref[...].astype(F32) * invf_ref[...]
    lane = lax.broadcasted_iota(jnp.int32, ang.shape, 1)
    s = jnp.sin(ang)
    cos_ref[...] = jnp.cos(ang)
    sin_ref[...] = jnp.where(lane < HEAD_DIM // 2, -s, s)


def _rope_tables(pos_col, invf_full, tm=2048):
    t = pos_col.shape[0]
    return pl.pallas_call(
        _rope_table_kernel,
        out_shape=(jax.ShapeDtypeStruct((t, HEAD_DIM), F32),
                   jax.ShapeDtypeStruct((t, HEAD_DIM), F32)),
        grid=(t // tm,),
        in_specs=[pl.BlockSpec((tm, 1), lambda i: (i, 0)),
                  pl.BlockSpec((1, HEAD_DIM), lambda i: (0, 0))],
        out_specs=(pl.BlockSpec((tm, HEAD_DIM), lambda i: (i, 0)),
                   pl.BlockSpec((tm, HEAD_DIM), lambda i: (i, 0))),
        compiler_params=_params(("arbitrary",)),
        name="rope_tables",
    )(pos_col, invf_full)


def _rmsnorm_kernel(x_ref, g_ref, o_ref):
    x = x_ref[...].astype(F32)
    ms = jnp.mean(x * x, axis=-1, keepdims=True)
    o_ref[...] = (x * lax.rsqrt(ms + RMS_EPS) * g_ref[...]).astype(o_ref.dtype)


def _rmsnorm(x, g_row, out_dtype, tm=256):
    t, d = x.shape
    return pl.pallas_call(
        _rmsnorm_kernel,
        out_shape=jax.ShapeDtypeStruct((t, d), out_dtype),
        grid=(t // tm,),
        in_specs=[pl.BlockSpec((tm, d), lambda i: (i, 0)),
                  pl.BlockSpec((1, d), lambda i: (0, 0))],
        out_specs=pl.BlockSpec((tm, d), lambda i: (i, 0)),
        compiler_params=_params(("arbitrary",)),
        name="rmsnorm",
    )(x, g_row)


def _mm_cast_kernel(a_ref, w_ref, o_ref):
    o_ref[...] = jnp.dot(a_ref[...], w_ref[...], preferred_element_type=F32).astype(o_ref.dtype)


def _mm_rope_kernel(a_ref, w_ref, cos_ref, sin_ref, cs_ref, o_ref):
    acc = jnp.dot(a_ref[...], w_ref[...], preferred_element_type=F32)
    cos = cos_ref[...]
    sin = sin_ref[...]
    for c in range(acc.shape[1] // HEAD_DIM):
        sl = slice(c * HEAD_DIM, (c + 1) * HEAD_DIM)
        xc = acc[:, sl]
        r = xc * cos + pltpu.roll(xc, HEAD_DIM // 2, 1) * sin
        o_ref[:, sl] = (r * cs_ref[:, sl]).astype(o_ref.dtype)


def _mm_sigmoid_kernel(a_ref, w_ref, b_ref, o_ref):
    acc = jnp.dot(a_ref[...], w_ref[...], preferred_element_type=F32)
    o_ref[...] = jax.nn.sigmoid(acc + b_ref[...]).astype(o_ref.dtype)


def _in_proj(u, w, col0, ncols, kind, extras=(), bm=1024, bn=512):
    t, k = u.shape
    joff = col0 // bn
    in_specs = [pl.BlockSpec((bm, k), lambda i, j: (i, 0)),
                pl.BlockSpec((k, bn), lambda i, j: (0, j + joff))]
    if kind == "cast":
        body = _mm_cast_kernel
    elif kind == "rope":
        body = _mm_rope_kernel
        in_specs += [pl.BlockSpec((bm, HEAD_DIM), lambda i, j: (i, 0)),
                     pl.BlockSpec((bm, HEAD_DIM), lambda i, j: (i, 0)),
                     pl.BlockSpec((1, bn), lambda i, j: (0, j))]
    else:
        body = _mm_sigmoid_kernel
        in_specs += [pl.BlockSpec((1, bn), lambda i, j: (0, j))]
    return pl.pallas_call(
        body,
        out_shape=jax.ShapeDtypeStruct((t, ncols), BF16),
        grid=(t // bm, ncols // bn),
        in_specs=in_specs,
        out_specs=pl.BlockSpec((bm, bn), lambda i, j: (i, j)),
        compiler_params=_params(("arbitrary", "arbitrary")),
        name="in_proj_" + kind,
    )(u, w, *extras)


def _dft_constants(seq):
    gd = FOURIER_GROUP_DIM
    jc = np.arange(gd, dtype=np.int64)
    ang_c = 2.0 * np.pi * ((jc[:, None] * jc[None, :]) % gd) / gd
    chan = np.concatenate([np.cos(ang_c), np.sin(ang_c)], axis=1) / math.sqrt(gd)
    js = np.arange(seq, dtype=np.int64)
    ang_s = 2.0 * np.pi * ((js[:, None] * js[None, :]) % seq) / seq
    seqm = np.concatenate([np.cos(ang_s), -np.sin(ang_s)], axis=1) / math.sqrt(seq)
    return jnp.asarray(chan, dtype=BF16), jnp.asarray(seqm, dtype=BF16)


def _fourier_kernel(f_ref, chan_ref, seqm_ref, o_ref, pq_ref):
    s = f_ref.shape[0]
    gd = FOURIER_GROUP_DIM
    pq = jnp.dot(f_ref[...], chan_ref[...], preferred_element_type=F32)
    pq_ref[0:s, :] = pq[:, 0:gd].astype(BF16)
    pq_ref[s:2 * s, :] = pq[:, gd:2 * gd].astype(BF16)
    o_ref[...] = jnp.dot(seqm_ref[...], pq_ref[...], preferred_element_type=F32).astype(o_ref.dtype)


def _fourier_mix(f, batch, seq, chan, seqm):
    t, width = f.shape
    gd = FOURIER_GROUP_DIM
    return pl.pallas_call(
        _fourier_kernel,
        out_shape=jax.ShapeDtypeStruct((t, width), BF16),
        grid=(batch, width // gd),
        in_specs=[pl.BlockSpec((seq, gd), lambda b, g: (b, g)),
                  pl.BlockSpec((gd, 2 * gd), lambda b, g: (0, 0)),
                  pl.BlockSpec((seq, 2 * seq), lambda b, g: (0, 0),
                               pipeline_mode=pl.Buffered(1))],
        out_specs=pl.BlockSpec((seq, gd), lambda b, g: (b, g)),
        scratch_shapes=[pltpu.VMEM((2 * seq, gd), BF16)],
        compiler_params=_params(("arbitrary", "arbitrary")),
        name="fourier_mix",
    )(f, chan, seqm)


def _attn_kernel(lam_ref, g_ref, q_ref, k_ref, v_ref, o_ref, s0, s1, p0, p1, l0, l1, *, lam_init, tq):
    nq = q_ref.shape[0] // tq
    s_buf, p_buf, l_buf = (s0, s1), (p0, p1), (l0, l1)
    lp = lam_ref[...]
    lam = (jnp.exp(jnp.sum(lp[0:1] * lp[1:2], axis=-1, keepdims=True))
           - jnp.exp(jnp.sum(lp[2:3] * lp[3:4], axis=-1, keepdims=True)) + lam_init)

    def scores(j, slot):
        q = q_ref[pl.ds(pl.multiple_of(j * tq, tq), tq), :]
        for c in range(2):
            sl = slice(c * HEAD_DIM, (c + 1) * HEAD_DIM)
            s_buf[slot][c] = lax.dot_general(q[:, sl], k_ref[:, sl], (((1,), (1,)), ((), ())),
                                             preferred_element_type=F32)

    def softmax(slot):
        for c in range(2):
            for r in range(0, tq, BF16_SUBLANES):
                rows = slice(r, r + BF16_SUBLANES)
                s = s_buf[slot][c, rows, :]
                p = jnp.exp2(s - jnp.max(s, axis=-1, keepdims=True))
                l_buf[slot][c, rows, :] = jnp.sum(p, axis=-1, keepdims=True)
                p_buf[slot][c, rows, :] = p.astype(BF16)

    def values(j, slot):
        o = [jnp.dot(p_buf[slot][c], v_ref[...], preferred_element_type=F32) / l_buf[slot][c]
             for c in range(2)]
        o = o[0] - lam * o[1]
        ms = jnp.mean(o * o, axis=-1, keepdims=True)
        o = o * lax.rsqrt(ms + RMS_EPS) * g_ref[...] * (1.0 - lam_init)
        o_ref[pl.ds(pl.multiple_of(j * tq, tq), tq), :] = o.astype(o_ref.dtype)

    scores(0, 0)
    scores(1, 1)
    softmax(0)

    def pair(jj, carry):
        j = 2 * jj + 1
        scores(j + 1, 0)
        softmax(1)
        values(j - 1, 0)
        scores(j + 2, 1)
        softmax(0)
        values(j, 1)
        return carry

    assert nq % 2 == 0 and nq >= 4
    lax.fori_loop(0, (nq - 2) // 2, pair, 0)
    softmax(1)
    values(nq - 2, 0)
    values(nq - 1, 1)


def _diff_attention(qk, v, lam_params, sub_g, batch, seq, lam_init, tq=256):
    t = qk.shape[0]
    hw = 2 * HEAD_DIM
    return pl.pallas_call(
        functools.partial(_attn_kernel, lam_init=lam_init, tq=tq),
        out_shape=jax.ShapeDtypeStruct((t, ATTN_WIDTH), BF16),
        grid=(batch, HEADS),
        in_specs=[pl.BlockSpec((4, HEAD_DIM), lambda b, h: (0, 0)),
                  pl.BlockSpec((1, hw), lambda b, h: (0, 0)),
                  pl.BlockSpec((seq, hw), lambda b, h: (b, h)),
                  pl.BlockSpec((seq, hw), lambda b, h: (b, HEADS + h)),
                  pl.BlockSpec((seq, hw), lambda b, h: (b, h))],
        out_specs=pl.BlockSpec((seq, hw), lambda b, h: (b, h)),
        scratch_shapes=[pltpu.VMEM((2, tq, seq), F32), pltpu.VMEM((2, tq, seq), F32),
                        pltpu.VMEM((2, tq, seq), BF16), pltpu.VMEM((2, tq, seq), BF16),
                        pltpu.VMEM((2, tq, 1), F32), pltpu.VMEM((2, tq, 1), F32)],
        compiler_params=_params(("arbitrary", "arbitrary")),
        name="diff_attention",
    )(lam_params, sub_g, qk, qk, v)


def _merge_kernel(yf_ref, o_ref_in, wf_ref, wa_ref, gf_ref, ga_ref, out_ref):
    y_f = jnp.dot(yf_ref[...], wf_ref[...], preferred_element_type=F32)
    y_a = jnp.dot(o_ref_in[...], wa_ref[...], preferred_element_type=F32)
    out_ref[...] = (gf_ref[...].astype(F32) * y_f + ga_ref[...].astype(F32) * y_a).astype(out_ref.dtype)


def _merge(yf, oa, wf, wa, gates, bm=1024, bn=512):
    t, kf = yf.shape
    ka = oa.shape[1]
    d = wf.shape[1]
    ga_off = d // bn
    return pl.pallas_call(
        _merge_kernel,
        out_shape=jax.ShapeDtypeStruct((t, d), BF16),
        grid=(t // bm, d // bn),
        in_specs=[pl.BlockSpec((bm, kf), lambda i, j: (i, 0)),
                  pl.BlockSpec((bm, ka), lambda i, j: (i, 0)),
                  pl.BlockSpec((kf, bn), lambda i, j: (0, j)),
                  pl.BlockSpec((ka, bn), lambda i, j: (0, j)),
                  pl.BlockSpec((bm, bn), lambda i, j: (i, j)),
                  pl.BlockSpec((bm, bn), lambda i, j: (i, j + ga_off))],
        out_specs=pl.BlockSpec((bm, bn), lambda i, j: (i, j)),
        compiler_params=_params(("arbitrary", "arbitrary")),
        name="merge_proj",
    )(yf, oa, wf, wa, gates, gates)


def _mm_residual_kernel(a_ref, w_ref, r_ref, o_ref):
    acc = jnp.dot(a_ref[...], w_ref[...], preferred_element_type=F32)
    o_ref[...] = r_ref[...] + acc


def _mm_residual(a, w, res, bm=512, bn=512):
    t, k = a.shape
    n = w.shape[1]
    return pl.pallas_call(
        _mm_residual_kernel,
        out_shape=jax.ShapeDtypeStruct((t, n), F32),
        grid=(t // bm, n // bn),
        in_specs=[pl.BlockSpec((bm, k), lambda i, j: (i, 0)),
                  pl.BlockSpec((k, bn), lambda i, j: (0, j)),
                  pl.BlockSpec((bm, bn), lambda i, j: (i, j))],
        out_specs=pl.BlockSpec((bm, bn), lambda i, j: (i, j)),
        compiler_params=_params(("arbitrary", "arbitrary")),
        name="mm_residual",
    )(a, w, res)


def _swiglu_kernel(u_ref, wg_ref, wu_ref, o_ref):
    u = u_ref[...]
    gate = jnp.dot(u, wg_ref[...], preferred_element_type=F32)
    up = jnp.dot(u, wu_ref[...], preferred_element_type=F32)
    o_ref[...] = (jax.nn.silu(gate) * up).astype(o_ref.dtype)


def _swiglu(u, wg, wu, bm=1024, bn=256):
    t, k = u.shape
    n = wg.shape[1]
    return pl.pallas_call(
        _swiglu_kernel,
        out_shape=jax.ShapeDtypeStruct((t, n), BF16),
        grid=(t // bm, n // bn),
        in_specs=[pl.BlockSpec((bm, k), lambda i, j: (i, 0)),
                  pl.BlockSpec((k, bn), lambda i, j: (0, j)),
                  pl.BlockSpec((k, bn), lambda i, j: (0, j))],
        out_specs=pl.BlockSpec((bm, bn), lambda i, j: (i, j)),
        compiler_params=_params(("arbitrary", "arbitrary")),
        name="swiglu_up",
    )(u, wg, wu)


def kernel(x, positions, norm_mix_g, w_in, b_gate, lambda_q1, lambda_k1, lambda_q2, lambda_k2,
           subln_g, w_fourier_out, w_attn_out, w_out, norm_ffn_g, w_ffn_gate, w_ffn_up,
           w_ffn_down, norm_final_g):
    batch, seq, d = x.shape
    depth = w_in.shape[0]
    t = batch * seq
    h = x.reshape(t, d)

    inv_freq = ROPE_THETA ** (-jnp.arange(0, HEAD_DIM, 2, dtype=F32) / HEAD_DIM)
    invf_full = jnp.concatenate([inv_freq, inv_freq]).reshape(1, HEAD_DIM)
    cos_t, sin_t = _rope_tables(positions.reshape(t, 1), invf_full)
    qk_scale = jnp.concatenate([jnp.full((1, ATTN_WIDTH), math.log2(math.e) / math.sqrt(HEAD_DIM), F32),
                                jnp.ones((1, ATTN_WIDTH), F32)], axis=1)
    chan, seqm = _dft_constants(seq)

    for l in range(depth):
        lam_init = 0.8 - 0.6 * math.exp(-0.3 * l)
        w_in_b = w_in[l].astype(BF16)
        u = _rmsnorm(h, norm_mix_g[l].reshape(1, d), BF16)
        f = _in_proj(u, w_in_b, 0, FOURIER_WIDTH, "cast")
        qk = _in_proj(u, w_in_b, FOURIER_WIDTH, 2 * ATTN_WIDTH, "rope", (cos_t, sin_t, qk_scale))
        v = _in_proj(u, w_in_b, FOURIER_WIDTH + 2 * ATTN_WIDTH, ATTN_WIDTH, "cast")
        gates = _in_proj(u, w_in_b, FOURIER_WIDTH + 3 * ATTN_WIDTH, 2 * d, "sigmoid",
                         (b_gate[l].reshape(1, 2 * d),))

        y_f = _fourier_mix(f, batch, seq, chan, seqm)
        lam_params = jnp.stack([lambda_q1[l], lambda_k1[l], lambda_q2[l], lambda_k2[l]]).astype(F32)
        o_a = _diff_attention(qk, v, lam_params, subln_g[l].reshape(1, 2 * HEAD_DIM).astype(F32),
                              batch, seq, lam_init)

        merged = _merge(y_f, o_a, w_fourier_out[l].astype(BF16), w_attn_out[l].astype(BF16), gates)
        h = _mm_residual(merged, w_out[l].astype(BF16), h, bm=1024)

        u2 = _rmsnorm(h, norm_ffn_g[l].reshape(1, d), BF16)
        hid = _swiglu(u2, w_ffn_gate[l].astype(BF16), w_ffn_up[l].astype(BF16))
        h = _mm_residual(hid, w_ffn_down[l].astype(BF16), h)

    out = _rmsnorm(h, norm_final_g.reshape(1, d), x.dtype)
    return out.reshape(batch, seq, d)
```

```python
import functools
import math

import numpy as np
import jax
import jax.numpy as jnp
from jax import lax
from jax.experimental import pallas as pl
from jax.experimental.pallas import tpu as pltpu

F32 = jnp.float32
BF16 = jnp.bfloat16

D_MODEL = 4096
FOURIER_WIDTH = D_MODEL // 2
FOURIER_GROUPS = 8
FOURIER_GROUP_DIM = FOURIER_WIDTH // FOURIER_GROUPS
HEAD_DIM = 128
HEADS = (D_MODEL // 2) // (2 * HEAD_DIM)
ATTN_WIDTH = HEADS * 2 * HEAD_DIM
FFN_HIDDEN = -(-8 * D_MODEL // (3 * 256)) * 256
ROPE_THETA = 10000.0
RMS_EPS = 1e-6

LANES = 128
BF16_SUBLANES = 16
VMEM_LIMIT = 56 * 1024 * 1024


def _params(sem, flags=None):
    return pltpu.CompilerParams(dimension_semantics=sem, vmem_limit_bytes=VMEM_LIMIT, flags=flags)


def _rope_table_kernel(pos_ref, invf_ref, cos_ref, sin_ref):
    ang = pos_ref[...].astype(F32) * invf_ref[...]
    lane = lax.broadcasted_iota(jnp.int32, ang.shape, 1)
    s = jnp.sin(ang)
    cos_ref[...] = jnp.cos(ang)
    sin_ref[...] = jnp.where(lane < HEAD_DIM // 2, -s, s)


def _rope_tables(pos_col, invf_full, tm=2048):
    t = pos_col.shape[0]
    return pl.pallas_call(
        _rope_table_kernel,
        out_shape=(jax.ShapeDtypeStruct((t, HEAD_DIM), F32),
                   jax.ShapeDtypeStruct((t, HEAD_DIM), F32)),
        grid=(t // tm,),
        in_specs=[pl.BlockSpec((tm, 1), lambda i: (i, 0)),
                  pl.BlockSpec((1, HEAD_DIM), lambda i: (0, 0))],
        out_specs=(pl.BlockSpec((tm, HEAD_DIM), lambda i: (i, 0)),
                   pl.BlockSpec((tm, HEAD_DIM), lambda i: (i, 0))),
        compiler_params=_params(("arbitrary",)),
        name="rope_tables",
    )(pos_col, invf_full)


def _rmsnorm_kernel(x_ref, g_ref, o_ref):
    x = x_ref[...].astype(F32)
    ms = jnp.mean(x * x, axis=-1, keepdims=True)
    o_ref[...] = (x * lax.rsqrt(ms + RMS_EPS) * g_ref[...]).astype(o_ref.dtype)


def _rmsnorm(x, g_row, out_dtype, tm=256):
    t, d = x.shape
    return pl.pallas_call(
        _rmsnorm_kernel,
        out_shape=jax.ShapeDtypeStruct((t, d), out_dtype),
        grid=(t // tm,),
        in_specs=[pl.BlockSpec((tm, d), lambda i: (i, 0)),
                  pl.BlockSpec((1, d), lambda i: (0, 0))],
        out_specs=pl.BlockSpec((tm, d), lambda i: (i, 0)),
        compiler_params=_params(("arbitrary",)),
        name="rmsnorm",
    )(x, g_row)


def _mm_cast_kernel(a_ref, w_ref, o_ref):
    o_ref[...] = jnp.dot(a_ref[...], w_ref[...], preferred_element_type=F32).astype(o_ref.dtype)


def _mm_rope_kernel(a_ref, w_ref, cos_ref, sin_ref, cs_ref, o_ref):
    acc = jnp.dot(a_ref[...], w_ref[...], preferred_element_type=F32)
    cos = cos_ref[...]
    sin = sin_ref[...]
    for c in range(acc.shape[1] // HEAD_DIM):
        sl = slice(c * HEAD_DIM, (c + 1) * HEAD_DIM)
        xc = acc[:, sl]
        r = xc * cos + pltpu.roll(xc, HEAD_DIM // 2, 1) * sin
        o_ref[:, sl] = (r * cs_ref[:, sl]).astype(o_ref.dtype)


def _mm_sigmoid_kernel(a_ref, w_ref, b_ref, o_ref):
    acc = jnp.dot(a_ref[...], w_ref[...], preferred_element_type=F32)
    o_ref[...] = jax.nn.sigmoid(acc + b_ref[...]).astype(o_ref.dtype)


def _in_proj(u, w, col0, ncols, kind, extras=(), bm=1024, bn=1024):
    t, k = u.shape
    joff = col0 // bn
    in_specs = [pl.BlockSpec((bm, k), lambda i, j: (i, 0)),
                pl.BlockSpec((k, bn), lambda i, j: (0, j + joff))]
    if kind == "cast":
        body = _mm_cast_kernel
    elif kind == "rope":
        body = _mm_rope_kernel
        in_specs += [pl.BlockSpec((bm, HEAD_DIM), lambda i, j: (i, 0)),
                     pl.BlockSpec((bm, HEAD_DIM), lambda i, j: (i, 0)),
                     pl.BlockSpec((1, bn), lambda i, j: (0, j))]
    else:
        body = _mm_sigmoid_kernel
        in_specs += [pl.BlockSpec((1, bn), lambda i, j: (0, j))]
    return pl.pallas_call(
        body,
        out_shape=jax.ShapeDtypeStruct((t, ncols), BF16),
        grid=(t // bm, ncols // bn),
        in_specs=in_specs,
        out_specs=pl.BlockSpec((bm, bn), lambda i, j: (i, j)),
        compiler_params=_params(("arbitrary", "arbitrary")),
        name="in_proj_" + kind,
    )(u, w, *extras)


def _dft_constants(seq):
    gd = FOURIER_GROUP_DIM
    jc = np.arange(gd, dtype=np.int64)
    ang_c = 2.0 * np.pi * ((jc[:, None] * jc[None, :]) % gd) / gd
    chan = np.concatenate([np.cos(ang_c), np.sin(ang_c)], axis=1) / math.sqrt(gd)
    js = np.arange(seq, dtype=np.int64)
    ang_s = 2.0 * np.pi * ((js[:, None] * js[None, :]) % seq) / seq
    seqm = np.concatenate([np.cos(ang_s), -np.sin(ang_s)], axis=1) / math.sqrt(seq)
    return jnp.asarray(chan, dtype=BF16), jnp.asarray(seqm, dtype=BF16)


def _fourier_kernel(f_ref, chan_ref, seqm_ref, o_ref, pq_ref):
    s = f_ref.shape[0]
    gd = FOURIER_GROUP_DIM
    pq = jnp.dot(f_ref[...], chan_ref[...], preferred_element_type=F32)
    pq_ref[0:s, :] = pq[:, 0:gd].astype(BF16)
    pq_ref[s:2 * s, :] = pq[:, gd:2 * gd].astype(BF16)
    o_ref[...] = jnp.dot(seqm_ref[...], pq_ref[...], preferred_element_type=F32).astype(o_ref.dtype)


def _fourier_mix(f, batch, seq, chan, seqm):
    t, width = f.shape
    gd = FOURIER_GROUP_DIM
    return pl.pallas_call(
        _fourier_kernel,
        out_shape=jax.ShapeDtypeStruct((t, width), BF16),
        grid=(batch, width // gd),
        in_specs=[pl.BlockSpec((seq, gd), lambda b, g: (b, g)),
                  pl.BlockSpec((gd, 2 * gd), lambda b, g: (0, 0)),
                  pl.BlockSpec((seq, 2 * seq), lambda b, g: (0, 0),
                               pipeline_mode=pl.Buffered(1))],
        out_specs=pl.BlockSpec((seq, gd), lambda b, g: (b, g)),
        scratch_shapes=[pltpu.VMEM((2 * seq, gd), BF16)],
        compiler_params=_params(("arbitrary", "arbitrary")),
        name="fourier_mix",
    )(f, chan, seqm)


def _attn_kernel(lam_ref, g_ref, q_ref, k_ref, v_ref, o_ref, s0, s1, p0, p1, l0, l1, *, lam_init, tq):
    nq = q_ref.shape[0] // tq
    s_buf, p_buf, l_buf = (s0, s1), (p0, p1), (l0, l1)
    lp = lam_ref[...]
    lam = (jnp.exp(jnp.sum(lp[0:1] * lp[1:2], axis=-1, keepdims=True))
           - jnp.exp(jnp.sum(lp[2:3] * lp[3:4], axis=-1, keepdims=True)) + lam_init)

    def scores(j, slot):
        q = q_ref[j * tq:(j + 1) * tq, :]
        for c in range(2):
            sl = slice(c * HEAD_DIM, (c + 1) * HEAD_DIM)
            s_buf[slot][c] = lax.dot_general(q[:, sl], k_ref[:, sl], (((1,), (1,)), ((), ())),
                                             preferred_element_type=F32)

    def softmax(slot):
        for c in range(2):
            for r in range(0, tq, BF16_SUBLANES):
                rows = slice(r, r + BF16_SUBLANES)
                s = s_buf[slot][c, rows, :]
                p = jnp.exp2(s - jnp.max(s, axis=-1, keepdims=True))
                l_buf[slot][c, rows, :] = jnp.broadcast_to(jnp.sum(p, axis=-1, keepdims=True),
                                                           (BF16_SUBLANES, LANES))
                p_buf[slot][c, rows, :] = p.astype(BF16)

    def values(j, slot):
        o = [jnp.dot(p_buf[slot][c], v_ref[...], preferred_element_type=F32)
             / jnp.concatenate([l_buf[slot][c]] * (v_ref.shape[1] // LANES), axis=1)
             for c in range(2)]
        o = o[0] - lam * o[1]
        ms = jnp.mean(o * o, axis=-1, keepdims=True)
        o = o * lax.rsqrt(ms + RMS_EPS) * g_ref[...] * (1.0 - lam_init)
        o_ref[j * tq:(j + 1) * tq, :] = o.astype(o_ref.dtype)

    scores(0, 0)
    for j in range(nq):
        if j + 1 < nq:
            scores(j + 1, (j + 1) % 2)
        softmax(j % 2)
        if j >= 1:
            values(j - 1, (j - 1) % 2)
    values(nq - 1, (nq - 1) % 2)


def _diff_attention(qk, v, lam_params, sub_g, batch, seq, lam_init, tq=256):
    t = qk.shape[0]
    hw = 2 * HEAD_DIM
    return pl.pallas_call(
        functools.partial(_attn_kernel, lam_init=lam_init, tq=tq),
        out_shape=jax.ShapeDtypeStruct((t, ATTN_WIDTH), BF16),
        grid=(batch, HEADS),
        in_specs=[pl.BlockSpec((4, HEAD_DIM), lambda b, h: (0, 0)),
                  pl.BlockSpec((1, hw), lambda b, h: (0, 0)),
                  pl.BlockSpec((seq, hw), lambda b, h: (b, h)),
                  pl.BlockSpec((seq, hw), lambda b, h: (b, HEADS + h)),
                  pl.BlockSpec((seq, hw), lambda b, h: (b, h))],
        out_specs=pl.BlockSpec((seq, hw), lambda b, h: (b, h)),
        scratch_shapes=[pltpu.VMEM((2, tq, seq), F32), pltpu.VMEM((2, tq, seq), F32),
                        pltpu.VMEM((2, tq, seq), BF16), pltpu.VMEM((2, tq, seq), BF16),
                        pltpu.VMEM((2, tq, LANES), F32), pltpu.VMEM((2, tq, LANES), F32)],
        compiler_params=_params(("arbitrary", "arbitrary")),
        name="diff_attention",
    )(lam_params, sub_g, qk, qk, v)


def _merge_kernel(yf_ref, o_ref_in, wf_ref, wa_ref, gf_ref, ga_ref, out_ref):
    y_f = jnp.dot(yf_ref[...], wf_ref[...], preferred_element_type=F32)
    y_a = jnp.dot(o_ref_in[...], wa_ref[...], preferred_element_type=F32)
    out_ref[...] = (gf_ref[...].astype(F32) * y_f + ga_ref[...].astype(F32) * y_a).astype(out_ref.dtype)


def _merge(yf, oa, wf, wa, gates, bm=1024, bn=512):
    t, kf = yf.shape
    ka = oa.shape[1]
    d = wf.shape[1]
    ga_off = d // bn
    return pl.pallas_call(
        _merge_kernel,
        out_shape=jax.ShapeDtypeStruct((t, d), BF16),
        grid=(t // bm, d // bn),
        in_specs=[pl.BlockSpec((bm, kf), lambda i, j: (i, 0)),
                  pl.BlockSpec((bm, ka), lambda i, j: (i, 0)),
                  pl.BlockSpec((kf, bn), lambda i, j: (0, j)),
                  pl.BlockSpec((ka, bn), lambda i, j: (0, j)),
                  pl.BlockSpec((bm, bn), lambda i, j: (i, j)),
                  pl.BlockSpec((bm, bn), lambda i, j: (i, j + ga_off))],
        out_specs=pl.BlockSpec((bm, bn), lambda i, j: (i, j)),
        compiler_params=_params(("arbitrary", "arbitrary")),
        name="merge_proj",
    )(yf, oa, wf, wa, gates, gates)


def _mm_residual_kernel(a_ref, w_ref, r_ref, o_ref):
    acc = jnp.dot(a_ref[...], w_ref[...], preferred_element_type=F32)
    o_ref[...] = r_ref[...] + acc


def _mm_residual(a, w, res, bm=512, bn=512):
    t, k = a.shape
    n = w.shape[1]
    return pl.pallas_call(
        _mm_residual_kernel,
        out_shape=jax.ShapeDtypeStruct((t, n), F32),
        grid=(t // bm, n // bn),
        in_specs=[pl.BlockSpec((bm, k), lambda i, j: (i, 0)),
                  pl.BlockSpec((k, bn), lambda i, j: (0, j)),
                  pl.BlockSpec((bm, bn), lambda i, j: (i, j))],
        out_specs=pl.BlockSpec((bm, bn), lambda i, j: (i, j)),
        compiler_params=_params(("arbitrary", "arbitrary")),
        name="mm_residual",
    )(a, w, res)


def _mm_residual_stats_kernel(a_ref, w_ref, r_ref, g_ref, h_ref, hg_ref, ssq_ref):
    h = r_ref[...] + jnp.dot(a_ref[...], w_ref[...], preferred_element_type=F32)
    h_ref[...] = h
    hg_ref[...] = (h * g_ref[...]).astype(hg_ref.dtype)

    @pl.when(pl.program_id(1) == 0)
    def _():
        ssq_ref[...] = jnp.zeros_like(ssq_ref)

    ssq_ref[...] += jnp.sum(h * h, axis=-1, keepdims=True)


def _mm_residual_stats(a, w, res, g_row, bm=1024, bn=512):
    t, k = a.shape
    n = w.shape[1]
    return pl.pallas_call(
        _mm_residual_stats_kernel,
        out_shape=(jax.ShapeDtypeStruct((t, n), F32),
                   jax.ShapeDtypeStruct((t, n), BF16),
                   jax.ShapeDtypeStruct((t, 1), F32)),
        grid=(t // bm, n // bn),
        in_specs=[pl.BlockSpec((bm, k), lambda i, j: (i, 0)),
                  pl.BlockSpec((k, bn), lambda i, j: (0, j)),
                  pl.BlockSpec((bm, bn), lambda i, j: (i, j)),
                  pl.BlockSpec((1, bn), lambda i, j: (0, j))],
        out_specs=(pl.BlockSpec((bm, bn), lambda i, j: (i, j)),
                   pl.BlockSpec((bm, bn), lambda i, j: (i, j)),
                   pl.BlockSpec((bm, 1), lambda i, j: (i, 0))),
        compiler_params=_params(("arbitrary", "arbitrary")),
        name="mm_residual_stats",
    )(a, w, res, g_row)


def _swiglu_kernel(hg_ref, ssq_ref, wg_ref, wu_ref, o_ref):
    hg = hg_ref[...]
    rstd = lax.rsqrt(ssq_ref[...] * (1.0 / hg.shape[1]) + RMS_EPS)
    gate = jnp.dot(hg, wg_ref[...], preferred_element_type=F32) * rstd
    up = jnp.dot(hg, wu_ref[...], preferred_element_type=F32) * rstd
    o_ref[...] = (jax.nn.silu(gate) * up).astype(o_ref.dtype)


def _swiglu(hg, ssq, wg, wu, bm=1024, bn=256):
    t, k = hg.shape
    n = wg.shape[1]
    return pl.pallas_call(
        _swiglu_kernel,
        out_shape=jax.ShapeDtypeStruct((t, n), BF16),
        grid=(t // bm, n // bn),
        in_specs=[pl.BlockSpec((bm, k), lambda i, j: (i, 0)),
                  pl.BlockSpec((bm, 1), lambda i, j: (i, 0)),
                  pl.BlockSpec((k, bn), lambda i, j: (0, j)),
                  pl.BlockSpec((k, bn), lambda i, j: (0, j))],
        out_specs=pl.BlockSpec((bm, bn), lambda i, j: (i, j)),
        compiler_params=_params(("arbitrary", "arbitrary")),
        name="swiglu_up",
    )(hg, ssq, wg, wu)


def kernel(x, positions, norm_mix_g, w_in, b_gate, lambda_q1, lambda_k1, lambda_q2, lambda_k2,
           subln_g, w_fourier_out, w_attn_out, w_out, norm_ffn_g, w_ffn_gate, w_ffn_up,
           w_ffn_down, norm_final_g):
    batch, seq, d = x.shape
    depth = w_in.shape[0]
    t = batch * seq
    h = x.reshape(t, d)

    inv_freq = ROPE_THETA ** (-jnp.arange(0, HEAD_DIM, 2, dtype=F32) / HEAD_DIM)
    invf_full = jnp.concatenate([inv_freq, inv_freq]).reshape(1, HEAD_DIM)
    cos_t, sin_t = _rope_tables(positions.reshape(t, 1), invf_full)
    qk_scale = jnp.concatenate([jnp.full((1, ATTN_WIDTH), math.log2(math.e) / math.sqrt(HEAD_DIM), F32),
                                jnp.ones((1, ATTN_WIDTH), F32)], axis=1)
    chan, seqm = _dft_constants(seq)

    for l in range(depth):
        lam_init = 0.8 - 0.6 * math.exp(-0.3 * l)
        w_in_b = w_in[l].astype(BF16)
        u = _rmsnorm(h, norm_mix_g[l].reshape(1, d), BF16)
        f = _in_proj(u, w_in_b, 0, FOURIER_WIDTH, "cast")
        qk = _in_proj(u, w_in_b, FOURIER_WIDTH, 2 * ATTN_WIDTH, "rope", (cos_t, sin_t, qk_scale))
        v = _in_proj(u, w_in_b, FOURIER_WIDTH + 2 * ATTN_WIDTH, ATTN_WIDTH, "cast")
        gates = _in_proj(u, w_in_b, FOURIER_WIDTH + 3 * ATTN_WIDTH, 2 * d, "sigmoid",
                         (b_gate[l].reshape(1, 2 * d),))

        y_f = _fourier_mix(f, batch, seq, chan, seqm)
        lam_params = jnp.stack([lambda_q1[l], lambda_k1[l], lambda_q2[l], lambda_k2[l]]).astype(F32)
        o_a = _diff_attention(qk, v, lam_params, subln_g[l].reshape(1, 2 * HEAD_DIM).astype(F32),
                              batch, seq, lam_init)

        merged = _merge(y_f, o_a, w_fourier_out[l].astype(BF16), w_attn_out[l].astype(BF16), gates)
        h, hg, ssq = _mm_residual_stats(merged, w_out[l].astype(BF16), h,
                                        norm_ffn_g[l].reshape(1, d).astype(F32))
        hid = _swiglu(hg, ssq, w_ffn_gate[l].astype(BF16), w_ffn_up[l].astype(BF16))
        h = _mm_residual(hid, w_ffn_down[l].astype(BF16), h)

    out = _rmsnorm(h, norm_final_g.reshape(1, d), x.dtype)
    return out.reshape(batch, seq, d)
```

```python
import functools
import math

import numpy as np
import jax
import jax.numpy as jnp
from jax import lax
from jax.experimental import pallas as pl
from jax.experimental.pallas import tpu as pltpu

F32 = jnp.float32
BF16 = jnp.bfloat16

D_MODEL = 4096
FOURIER_WIDTH = D_MODEL // 2
FOURIER_GROUPS = 8
FOURIER_GROUP_DIM = FOURIER_WIDTH // FOURIER_GROUPS
HEAD_DIM = 128
HEADS = (D_MODEL // 2) // (2 * HEAD_DIM)
ATTN_WIDTH = HEADS * 2 * HEAD_DIM
FFN_HIDDEN = -(-8 * D_MODEL // (3 * 256)) * 256
ROPE_THETA = 10000.0
RMS_EPS = 1e-6

LANES = 128
BF16_SUBLANES = 16
VMEM_LIMIT = 56 * 1024 * 1024


def _params(sem, flags=None):
    return pltpu.CompilerParams(dimension_semantics=sem, vmem_limit_bytes=VMEM_LIMIT, flags=flags)


def _rider_view(w, steps):
    r, c = w.shape
    for rows, cols in ((r, c), (c, r)):
        if rows % (steps * BF16_SUBLANES) == 0 and cols % LANES == 0:
            return w.reshape(rows, cols)
    raise ValueError(f"cannot split weight {w.shape} into {steps} row blocks")


def _rider_specs(views, steps, inner):
    in_specs, out_shapes, out_specs = [], [], []
    for w in views:
        block = (w.shape[0] // steps, w.shape[1])
        in_specs.append(pl.BlockSpec(block, lambda a, b: (a * inner + b, 0)))
        out_specs.append(pl.BlockSpec(block, lambda a, b: (a * inner + b, 0)))
        out_shapes.append(jax.ShapeDtypeStruct(w.shape, BF16))
    return in_specs, out_shapes, out_specs


def _cast_riders(rider_in, rider_out):
    for w_ref, o_ref in zip(rider_in, rider_out):
        o_ref[...] = w_ref[...].astype(o_ref.dtype)


def _rope_table_kernel(pos_ref, invf_ref, cos_ref, sin_ref):
    ang = pos_ref[...].astype(F32) * invf_ref[...]
    lane = lax.broadcasted_iota(jnp.int32, ang.shape, 1)
    s = jnp.sin(ang)
    cos_ref[...] = jnp.cos(ang)
    sin_ref[...] = jnp.where(lane < HEAD_DIM // 2, -s, s)


def _rope_tables(pos_col, invf_full, tm=2048):
    t = pos_col.shape[0]
    return pl.pallas_call(
        _rope_table_kernel,
        out_shape=(jax.ShapeDtypeStruct((t, HEAD_DIM), F32),
                   jax.ShapeDtypeStruct((t, HEAD_DIM), F32)),
        grid=(t // tm,),
        in_specs=[pl.BlockSpec((tm, 1), lambda i: (i, 0)),
                  pl.BlockSpec((1, HEAD_DIM), lambda i: (0, 0))],
        out_specs=(pl.BlockSpec((tm, HEAD_DIM), lambda i: (i, 0)),
                   pl.BlockSpec((tm, HEAD_DIM), lambda i: (i, 0))),
        compiler_params=_params(("arbitrary",)),
        name="rope_tables",
    )(pos_col, invf_full)


def _rmsnorm_kernel(x_ref, g_ref, o_ref):
    x = x_ref[...].astype(F32)
    ms = jnp.mean(x * x, axis=-1, keepdims=True)
    o_ref[...] = (x * lax.rsqrt(ms + RMS_EPS) * g_ref[...]).astype(o_ref.dtype)


def _rmsnorm(x, g_row, out_dtype, tm=256):
    t, d = x.shape
    return pl.pallas_call(
        _rmsnorm_kernel,
        out_shape=jax.ShapeDtypeStruct((t, d), out_dtype),
        grid=(t // tm,),
        in_specs=[pl.BlockSpec((tm, d), lambda i: (i, 0)),
                  pl.BlockSpec((1, d), lambda i: (0, 0))],
        out_specs=pl.BlockSpec((tm, d), lambda i: (i, 0)),
        compiler_params=_params(("arbitrary",)),
        name="rmsnorm",
    )(x, g_row)


def _mm_cast_kernel(a_ref, w_ref, o_ref):
    o_ref[...] = jnp.dot(a_ref[...], w_ref[...], preferred_element_type=F32).astype(o_ref.dtype)


def _mm_rope_kernel(a_ref, w_ref, cos_ref, sin_ref, cs_ref, o_ref):
    acc = jnp.dot(a_ref[...], w_ref[...], preferred_element_type=F32)
    cos = cos_ref[...]
    sin = sin_ref[...]
    for c in range(acc.shape[1] // HEAD_DIM):
        sl = slice(c * HEAD_DIM, (c + 1) * HEAD_DIM)
        xc = acc[:, sl]
        r = xc * cos + pltpu.roll(xc, HEAD_DIM // 2, 1) * sin
        o_ref[:, sl] = (r * cs_ref[:, sl]).astype(o_ref.dtype)


def _mm_sigmoid_kernel(a_ref, w_ref, b_ref, o_ref):
    acc = jnp.dot(a_ref[...], w_ref[...], preferred_element_type=F32)
    o_ref[...] = jax.nn.sigmoid(acc + b_ref[...]).astype(o_ref.dtype)


def _in_proj(u, w, col0, ncols, kind, extras=(), bm=1024, bn=1024):
    t, k = u.shape
    joff = col0 // bn
    in_specs = [pl.BlockSpec((bm, k), lambda i, j: (i, 0)),
                pl.BlockSpec((k, bn), lambda i, j: (0, j + joff))]
    if kind == "cast":
        body = _mm_cast_kernel
    elif kind == "rope":
        body = _mm_rope_kernel
        in_specs += [pl.BlockSpec((bm, HEAD_DIM), lambda i, j: (i, 0)),
                     pl.BlockSpec((bm, HEAD_DIM), lambda i, j: (i, 0)),
                     pl.BlockSpec((1, bn), lambda i, j: (0, j))]
    else:
        body = _mm_sigmoid_kernel
        in_specs += [pl.BlockSpec((1, bn), lambda i, j: (0, j))]
    return pl.pallas_call(
        body,
        out_shape=jax.ShapeDtypeStruct((t, ncols), BF16),
        grid=(t // bm, ncols // bn),
        in_specs=in_specs,
        out_specs=pl.BlockSpec((bm, bn), lambda i, j: (i, j)),
        compiler_params=_params(("arbitrary", "arbitrary")),
        name="in_proj_" + kind,
    )(u, w, *extras)


def _dft_constants(seq):
    gd = FOURIER_GROUP_DIM
    jc = np.arange(gd, dtype=np.int64)
    ang_c = 2.0 * np.pi * ((jc[:, None] * jc[None, :]) % gd) / gd
    chan = np.concatenate([np.cos(ang_c), np.sin(ang_c)], axis=1) / math.sqrt(gd)
    js = np.arange(seq, dtype=np.int64)
    ang_s = 2.0 * np.pi * ((js[:, None] * js[None, :]) % seq) / seq
    seqm = np.concatenate([np.cos(ang_s), -np.sin(ang_s)], axis=1) / math.sqrt(seq)
    return jnp.asarray(chan, dtype=BF16), jnp.asarray(seqm, dtype=BF16)


def _fourier_kernel(*refs, n_riders):
    f_ref, chan_ref, seqm_ref = refs[:3]
    o_ref, pq_ref = refs[3 + n_riders], refs[-1]
    _cast_riders(refs[3:3 + n_riders], refs[4 + n_riders:4 + 2 * n_riders])
    s = f_ref.shape[0]
    gd = FOURIER_GROUP_DIM
    pq = jnp.dot(f_ref[...], chan_ref[...], preferred_element_type=F32)
    pq_ref[0:s, :] = pq[:, 0:gd].astype(BF16)
    pq_ref[s:2 * s, :] = pq[:, gd:2 * gd].astype(BF16)
    o_ref[...] = jnp.dot(seqm_ref[...], pq_ref[...], preferred_element_type=F32).astype(o_ref.dtype)


def _fourier_mix(f, batch, seq, chan, seqm, riders):
    t, width = f.shape
    gd = FOURIER_GROUP_DIM
    groups = width // gd
    views = [_rider_view(w, batch * groups) for w in riders]
    r_in, r_shapes, r_out = _rider_specs(views, batch * groups, groups)
    outs = pl.pallas_call(
        functools.partial(_fourier_kernel, n_riders=len(riders)),
        out_shape=[jax.ShapeDtypeStruct((t, width), BF16)] + r_shapes,
        grid=(batch, groups),
        in_specs=[pl.BlockSpec((seq, gd), lambda b, g: (b, g)),
                  pl.BlockSpec((gd, 2 * gd), lambda b, g: (0, 0)),
                  pl.BlockSpec((seq, 2 * seq), lambda b, g: (0, 0),
                               pipeline_mode=pl.Buffered(1))] + r_in,
        out_specs=[pl.BlockSpec((seq, gd), lambda b, g: (b, g))] + r_out,
        scratch_shapes=[pltpu.VMEM((2 * seq, gd), BF16)],
        compiler_params=_params(("arbitrary", "arbitrary")),
        name="fourier_mix",
    )(f, chan, seqm, *views)
    return outs[0], [o.reshape(w.shape) for o, w in zip(outs[1:], riders)]


def _attn_kernel(*refs, lam_init, tq, n_riders):
    lam_ref, g_ref, q_ref, k_ref, v_ref = refs[:5]
    o_ref = refs[5 + n_riders]
    s0, s1, p0, p1, l0, l1 = refs[6 + 2 * n_riders:]
    _cast_riders(refs[5:5 + n_riders], refs[6 + n_riders:6 + 2 * n_riders])
    nq = q_ref.shape[0] // tq
    s_buf, p_buf, l_buf = (s0, s1), (p0, p1), (l0, l1)
    lp = lam_ref[...]
    lam = (jnp.exp(jnp.sum(lp[0:1] * lp[1:2], axis=-1, keepdims=True))
           - jnp.exp(jnp.sum(lp[2:3] * lp[3:4], axis=-1, keepdims=True)) + lam_init)

    def scores(j, slot):
        q = q_ref[j * tq:(j + 1) * tq, :]
        for c in range(2):
            sl = slice(c * HEAD_DIM, (c + 1) * HEAD_DIM)
            s_buf[slot][c] = lax.dot_general(q[:, sl], k_ref[:, sl], (((1,), (1,)), ((), ())),
                                             preferred_element_type=F32)

    def softmax(slot):
        for c in range(2):
            for r in range(0, tq, BF16_SUBLANES):
                rows = slice(r, r + BF16_SUBLANES)
                s = s_buf[slot][c, rows, :]
                p = jnp.exp2(s - jnp.max(s, axis=-1, keepdims=True))
                l_buf[slot][c, rows, :] = jnp.broadcast_to(jnp.sum(p, axis=-1, keepdims=True),
                                                           (BF16_SUBLANES, LANES))
                p_buf[slot][c, rows, :] = p.astype(BF16)

    def values(j, slot):
        o = [jnp.dot(p_buf[slot][c], v_ref[...], preferred_element_type=F32)
             / jnp.concatenate([l_buf[slot][c]] * (v_ref.shape[1] // LANES), axis=1)
             for c in range(2)]
        o = o[0] - lam * o[1]
        ms = jnp.mean(o * o, axis=-1, keepdims=True)
        o = o * lax.rsqrt(ms + RMS_EPS) * g_ref[...] * (1.0 - lam_init)
        o_ref[j * tq:(j + 1) * tq, :] = o.astype(o_ref.dtype)

    scores(0, 0)
    for j in range(nq):
        if j + 1 < nq:
            scores(j + 1, (j + 1) % 2)
        softmax(j % 2)
        if j >= 1:
            values(j - 1, (j - 1) % 2)
    values(nq - 1, (nq - 1) % 2)


def _diff_attention(qk, v, lam_params, sub_g, batch, seq, lam_init, riders, tq=256):
    t = qk.shape[0]
    hw = 2 * HEAD_DIM
    views = [_rider_view(w, batch * HEADS) for w in riders]
    r_in, r_shapes, r_out = _rider_specs(views, batch * HEADS, HEADS)
    outs = pl.pallas_call(
        functools.partial(_attn_kernel, lam_init=lam_init, tq=tq, n_riders=len(riders)),
        out_shape=[jax.ShapeDtypeStruct((t, ATTN_WIDTH), BF16)] + r_shapes,
        grid=(batch, HEADS),
        in_specs=[pl.BlockSpec((4, HEAD_DIM), lambda b, h: (0, 0)),
                  pl.BlockSpec((1, hw), lambda b, h: (0, 0)),
                  pl.BlockSpec((seq, hw), lambda b, h: (b, h)),
                  pl.BlockSpec((seq, hw), lambda b, h: (b, HEADS + h)),
                  pl.BlockSpec((seq, hw), lambda b, h: (b, h))] + r_in,
        out_specs=[pl.BlockSpec((seq, hw), lambda b, h: (b, h))] + r_out,
        scratch_shapes=[pltpu.VMEM((2, tq, seq), F32), pltpu.VMEM((2, tq, seq), F32),
                        pltpu.VMEM((2, tq, seq), BF16), pltpu.VMEM((2, tq, seq), BF16),
                        pltpu.VMEM((2, tq, LANES), F32), pltpu.VMEM((2, tq, LANES), F32)],
        compiler_params=_params(("arbitrary", "arbitrary")),
        name="diff_attention",
    )(lam_params, sub_g, qk, qk, v, *views)
    return outs[0], [o.reshape(w.shape) for o, w in zip(outs[1:], riders)]


def _merge_kernel(yf_ref, o_ref_in, wf_ref, wa_ref, gf_ref, ga_ref, out_ref):
    y_f = jnp.dot(yf_ref[...], wf_ref[...], preferred_element_type=F32)
    y_a = jnp.dot(o_ref_in[...], wa_ref[...], preferred_element_type=F32)
    out_ref[...] = (gf_ref[...].astype(F32) * y_f + ga_ref[...].astype(F32) * y_a).astype(out_ref.dtype)


def _merge(yf, oa, wf, wa, gates, bm=1024, bn=512):
    t, kf = yf.shape
    ka = oa.shape[1]
    d = wf.shape[1]
    ga_off = d // bn
    return pl.pallas_call(
        _merge_kernel,
        out_shape=jax.ShapeDtypeStruct((t, d), BF16),
        grid=(t // bm, d // bn),
        in_specs=[pl.BlockSpec((bm, kf), lambda i, j: (i, 0)),
                  pl.BlockSpec((bm, ka), lambda i, j: (i, 0)),
                  pl.BlockSpec((kf, bn), lambda i, j: (0, j)),
                  pl.BlockSpec((ka, bn), lambda i, j: (0, j)),
                  pl.BlockSpec((bm, bn), lambda i, j: (i, j)),
                  pl.BlockSpec((bm, bn), lambda i, j: (i, j + ga_off))],
        out_specs=pl.BlockSpec((bm, bn), lambda i, j: (i, j)),
        compiler_params=_params(("arbitrary", "arbitrary")),
        name="merge_proj",
    )(yf, oa, wf, wa, gates, gates)


def _mm_residual_kernel(a_ref, w_ref, r_ref, o_ref):
    acc = jnp.dot(a_ref[...], w_ref[...], preferred_element_type=F32)
    o_ref[...] = r_ref[...] + acc


def _mm_residual(a, w, res, bm=512, bn=512):
    t, k = a.shape
    n = w.shape[1]
    return pl.pallas_call(
        _mm_residual_kernel,
        out_shape=jax.ShapeDtypeStruct((t, n), F32),
        grid=(t // bm, n // bn),
        in_specs=[pl.BlockSpec((bm, k), lambda i, j: (i, 0)),
                  pl.BlockSpec((k, bn), lambda i, j: (0, j)),
                  pl.BlockSpec((bm, bn), lambda i, j: (i, j))],
        out_specs=pl.BlockSpec((bm, bn), lambda i, j: (i, j)),
        compiler_params=_params(("arbitrary", "arbitrary")),
        name="mm_residual",
    )(a, w, res)


def _mm_residual_stats_kernel(a_ref, w_ref, r_ref, g_ref, h_ref, hg_ref, ssq_ref):
    h = r_ref[...] + jnp.dot(a_ref[...], w_ref[...], preferred_element_type=F32)
    h_ref[...] = h
    hg_ref[...] = (h * g_ref[...]).astype(hg_ref.dtype)

    @pl.when(pl.program_id(1) == 0)
    def _():
        ssq_ref[...] = jnp.zeros_like(ssq_ref)

    ssq_ref[...] += jnp.sum(h * h, axis=-1, keepdims=True)


def _mm_residual_stats(a, w, res, g_row, bm=1024, bn=512):
    t, k = a.shape
    n = w.shape[1]
    return pl.pallas_call(
        _mm_residual_stats_kernel,
        out_shape=(jax.ShapeDtypeStruct((t, n), F32),
                   jax.ShapeDtypeStruct((t, n), BF16),
                   jax.ShapeDtypeStruct((t, 1), F32)),
        grid=(t // bm, n // bn),
        in_specs=[pl.BlockSpec((bm, k), lambda i, j: (i, 0)),
                  pl.BlockSpec((k, bn), lambda i, j: (0, j)),
                  pl.BlockSpec((bm, bn), lambda i, j: (i, j)),
                  pl.BlockSpec((1, bn), lambda i, j: (0, j))],
        out_specs=(pl.BlockSpec((bm, bn), lambda i, j: (i, j)),
                   pl.BlockSpec((bm, bn), lambda i, j: (i, j)),
                   pl.BlockSpec((bm, 1), lambda i, j: (i, 0))),
        compiler_params=_params(("arbitrary", "arbitrary")),
        name="mm_residual_stats",
    )(a, w, res, g_row)


def _swiglu_kernel(hg_ref, ssq_ref, wg_ref, wu_ref, o_ref):
    hg = hg_ref[...]
    rstd = lax.rsqrt(ssq_ref[...] * (1.0 / hg.shape[1]) + RMS_EPS)
    gate = jnp.dot(hg, wg_ref[...], preferred_element_type=F32) * rstd
    up = jnp.dot(hg, wu_ref[...], preferred_element_type=F32) * rstd
    o_ref[...] = (jax.nn.silu(gate) * up).astype(o_ref.dtype)


def _swiglu(hg, ssq, wg, wu, bm=1024, bn=256):
    t, k = hg.shape
    n = wg.shape[1]
    return pl.pallas_call(
        _swiglu_kernel,
        out_shape=jax.ShapeDtypeStruct((t, n), BF16),
        grid=(t // bm, n // bn),
        in_specs=[pl.BlockSpec((bm, k), lambda i, j: (i, 0)),
                  pl.BlockSpec((bm, 1), lambda i, j: (i, 0)),
                  pl.BlockSpec((k, bn), lambda i, j: (0, j)),
                  pl.BlockSpec((k, bn), lambda i, j: (0, j))],
        out_specs=pl.BlockSpec((bm, bn), lambda i, j: (i, j)),
        compiler_params=_params(("arbitrary", "arbitrary")),
        name="swiglu_up",
    )(hg, ssq, wg, wu)


def kernel(x, positions, norm_mix_g, w_in, b_gate, lambda_q1, lambda_k1, lambda_q2, lambda_k2,
           subln_g, w_fourier_out, w_attn_out, w_out, norm_ffn_g, w_ffn_gate, w_ffn_up,
           w_ffn_down, norm_final_g):
    batch, seq, d = x.shape
    depth = w_in.shape[0]
    t = batch * seq
    h = x.reshape(t, d)

    inv_freq = ROPE_THETA ** (-jnp.arange(0, HEAD_DIM, 2, dtype=F32) / HEAD_DIM)
    invf_full = jnp.concatenate([inv_freq, inv_freq]).reshape(1, HEAD_DIM)
    cos_t, sin_t = _rope_tables(positions.reshape(t, 1), invf_full)
    qk_scale = jnp.concatenate([jnp.full((1, ATTN_WIDTH), math.log2(math.e) / math.sqrt(HEAD_DIM), F32),
                                jnp.ones((1, ATTN_WIDTH), F32)], axis=1)
    chan, seqm = _dft_constants(seq)

    for l in range(depth):
        lam_init = 0.8 - 0.6 * math.exp(-0.3 * l)
        w_in_b = w_in[l].astype(BF16)
        u = _rmsnorm(h, norm_mix_g[l].reshape(1, d), BF16)
        f = _in_proj(u, w_in_b, 0, FOURIER_WIDTH, "cast")
        qk = _in_proj(u, w_in_b, FOURIER_WIDTH, 2 * ATTN_WIDTH, "rope", (cos_t, sin_t, qk_scale))
        v = _in_proj(u, w_in_b, FOURIER_WIDTH + 2 * ATTN_WIDTH, ATTN_WIDTH, "cast")
        gates = _in_proj(u, w_in_b, FOURIER_WIDTH + 3 * ATTN_WIDTH, 2 * d, "sigmoid",
                         (b_gate[l].reshape(1, 2 * d),))

        y_f, (w_f_b, w_a_b, w_out_b, w_down_b) = _fourier_mix(
            f, batch, seq, chan, seqm, [w_fourier_out[l], w_attn_out[l], w_out[l], w_ffn_down[l]])
        lam_params = jnp.stack([lambda_q1[l], lambda_k1[l], lambda_q2[l], lambda_k2[l]]).astype(F32)
        o_a, (w_gate_b, w_up_b) = _diff_attention(
            qk, v, lam_params, subln_g[l].reshape(1, 2 * HEAD_DIM).astype(F32), batch, seq, lam_init,
            [w_ffn_gate[l], w_ffn_up[l]])

        merged = _merge(y_f, o_a, w_f_b, w_a_b, gates)
        h, hg, ssq = _mm_residual_stats(merged, w_out_b, h, norm_ffn_g[l].reshape(1, d).astype(F32))
        hid = _swiglu(hg, ssq, w_gate_b, w_up_b)
        h = _mm_residual(hid, w_down_b, h)

    out = _rmsnorm(h, norm_final_g.reshape(1, d), x.dtype)
    return out.reshape(batch, seq, d)
```

```python
import functools
import math

import numpy as np
import jax
import jax.numpy as jnp
from jax import lax
from jax.experimental import pallas as pl
from jax.experimental.pallas import tpu as pltpu

F32 = jnp.float32
BF16 = jnp.bfloat16

D_MODEL = 4096
FOURIER_WIDTH = D_MODEL // 2
FOURIER_GROUPS = 8
FOURIER_GROUP_DIM = FOURIER_WIDTH // FOURIER_GROUPS
HEAD_DIM = 128
HEADS = (D_MODEL // 2) // (2 * HEAD_DIM)
ATTN_WIDTH = HEADS * 2 * HEAD_DIM
FFN_HIDDEN = -(-8 * D_MODEL // (3 * 256)) * 256
ROPE_THETA = 10000.0
RMS_EPS = 1e-6

LANES = 128
BF16_SUBLANES = 16
VMEM_LIMIT = 56 * 1024 * 1024


def _params(sem):
    return pltpu.CompilerParams(dimension_semantics=sem, vmem_limit_bytes=VMEM_LIMIT)


def _sigmoid(x):
    return 0.5 * jnp.tanh(0.5 * x) + 0.5


def _rider_rows(rows, steps):
    for nblk in range(steps, 0, -1):
        if rows % nblk == 0 and (rows // nblk) % BF16_SUBLANES == 0:
            return rows // nblk
    raise ValueError(f"cannot split {rows} rows into at most {steps} bf16-tileable blocks")


def _rider_specs(weights, steps, inner):
    in_specs, out_shapes, out_specs = [], [], []
    for w in weights:
        rb = _rider_rows(w.shape[0], steps)
        last = w.shape[0] // rb - 1
        index = lambda a, b, last=last: (jnp.minimum(a * inner + b, last), 0)
        in_specs.append(pl.BlockSpec((rb, w.shape[1]), index))
        out_specs.append(pl.BlockSpec((rb, w.shape[1]), index))
        out_shapes.append(jax.ShapeDtypeStruct(w.shape, BF16))
    return in_specs, out_shapes, out_specs


def _cast_riders(rider_in, rider_out):
    for w_ref, o_ref in zip(rider_in, rider_out):
        o_ref[...] = w_ref[...].astype(o_ref.dtype)


def _rope_tables(pos_ref, invf_ref, cos_ref, sin_ref):
    ang = pos_ref[...].astype(F32) * invf_ref[...]
    lane = lax.broadcasted_iota(jnp.int32, ang.shape, 1)
    s = jnp.sin(ang)
    cos_ref[...] = jnp.cos(ang)
    sin_ref[...] = jnp.where(lane < HEAD_DIM // 2, -s, s)


def _rmsnorm_kernel(x_ref, g_ref, o_ref):
    x = x_ref[...].astype(F32)
    ms = jnp.mean(x * x, axis=-1, keepdims=True)
    o_ref[...] = (x * lax.rsqrt(ms + RMS_EPS) * g_ref[...]).astype(o_ref.dtype)


def _rmsnorm_rope_kernel(x_ref, g_ref, pos_ref, invf_ref, o_ref, cos_ref, sin_ref):
    _rmsnorm_kernel(x_ref, g_ref, o_ref)
    _rope_tables(pos_ref, invf_ref, cos_ref, sin_ref)


def _rmsnorm(x, g_row, out_dtype, rope=None, tm=256):
    t, d = x.shape
    row = lambda i: (i, 0)
    fixed = lambda i: (0, 0)
    in_specs = [pl.BlockSpec((tm, d), row), pl.BlockSpec((1, d), fixed)]
    out_shape = jax.ShapeDtypeStruct((t, d), out_dtype)
    out_specs = pl.BlockSpec((tm, d), row)
    if rope is None:
        body, args = _rmsnorm_kernel, (x, g_row)
    else:
        body, args = _rmsnorm_rope_kernel, (x, g_row, *rope)
        in_specs += [pl.BlockSpec((tm, 1), row), pl.BlockSpec((1, HEAD_DIM), fixed)]
        out_shape = (out_shape,) + (jax.ShapeDtypeStruct((t, HEAD_DIM), F32),) * 2
        out_specs = (out_specs,) + (pl.BlockSpec((tm, HEAD_DIM), row),) * 2
    return pl.pallas_call(
        body,
        out_shape=out_shape,
        grid=(t // tm,),
        in_specs=in_specs,
        out_specs=out_specs,
        compiler_params=_params(("arbitrary",)),
        name="rmsnorm",
    )(*args)


def _mm_cast_kernel(a_ref, w_ref, o_ref):
    o_ref[...] = jnp.dot(a_ref[...], w_ref[...], preferred_element_type=F32).astype(o_ref.dtype)


def _mm_rope_kernel(a_ref, w_ref, cos_ref, sin_ref, cs_ref, o_ref):
    acc = jnp.dot(a_ref[...], w_ref[...], preferred_element_type=F32)
    cos = cos_ref[...]
    sin = sin_ref[...]
    for c in range(acc.shape[1] // HEAD_DIM):
        sl = slice(c * HEAD_DIM, (c + 1) * HEAD_DIM)
        xc = acc[:, sl]
        r = xc * cos + pltpu.roll(xc, HEAD_DIM // 2, 1) * sin
        o_ref[:, sl] = (r * cs_ref[:, sl]).astype(o_ref.dtype)


def _mm_sigmoid_kernel(a_ref, w_ref, b_ref, o_ref):
    acc = jnp.dot(a_ref[...], w_ref[...], preferred_element_type=F32)
    o_ref[...] = _sigmoid(acc + b_ref[...]).astype(o_ref.dtype)


def _in_proj(u, w, col0, ncols, kind, extras=(), bm=1024, bn=1024):
    t, k = u.shape
    joff = col0 // bn
    in_specs = [pl.BlockSpec((bm, k), lambda i, j: (i, 0)),
                pl.BlockSpec((k, bn), lambda i, j: (0, j + joff))]
    if kind == "cast":
        body = _mm_cast_kernel
    elif kind == "rope":
        body = _mm_rope_kernel
        in_specs += [pl.BlockSpec((bm, HEAD_DIM), lambda i, j: (i, 0)),
                     pl.BlockSpec((bm, HEAD_DIM), lambda i, j: (i, 0)),
                     pl.BlockSpec((1, bn), lambda i, j: (0, j))]
    else:
        body = _mm_sigmoid_kernel
        in_specs += [pl.BlockSpec((1, bn), lambda i, j: (0, j))]
    return pl.pallas_call(
        body,
        out_shape=jax.ShapeDtypeStruct((t, ncols), BF16),
        grid=(t // bm, ncols // bn),
        in_specs=in_specs,
        out_specs=pl.BlockSpec((bm, bn), lambda i, j: (i, j)),
        compiler_params=_params(("arbitrary", "arbitrary")),
        name="in_proj_" + kind,
    )(u, w, *extras)


def _dft_constants(seq):
    gd = FOURIER_GROUP_DIM
    jc = np.arange(gd, dtype=np.int64)
    ang_c = 2.0 * np.pi * ((jc[:, None] * jc[None, :]) % gd) / gd
    chan = np.concatenate([np.cos(ang_c), np.sin(ang_c)], axis=1) / math.sqrt(gd)
    js = np.arange(seq, dtype=np.int64)
    ang_s = 2.0 * np.pi * ((js[:, None] * js[None, :]) % seq) / seq
    seqm = np.concatenate([np.cos(ang_s), -np.sin(ang_s)], axis=1) / math.sqrt(seq)
    return jnp.asarray(chan, dtype=BF16), jnp.asarray(seqm, dtype=BF16)


def _fourier_kernel(*refs, n_riders):
    f_ref, chan_ref, seqm_ref = refs[:3]
    o_ref, pq_ref = refs[3 + n_riders], refs[-1]
    _cast_riders(refs[3:3 + n_riders], refs[4 + n_riders:4 + 2 * n_riders])
    s = f_ref.shape[0]
    gd = FOURIER_GROUP_DIM
    pq = jnp.dot(f_ref[...], chan_ref[...], preferred_element_type=F32)
    pq_ref[0:s, :] = pq[:, 0:gd].astype(BF16)
    pq_ref[s:2 * s, :] = pq[:, gd:2 * gd].astype(BF16)
    o_ref[...] = jnp.dot(seqm_ref[...], pq_ref[...], preferred_element_type=F32).astype(o_ref.dtype)


def _fourier_mix(f, batch, seq, chan, seqm, riders):
    t, width = f.shape
    gd = FOURIER_GROUP_DIM
    groups = width // gd
    r_in, r_shapes, r_out = _rider_specs(riders, batch * groups, groups)
    outs = pl.pallas_call(
        functools.partial(_fourier_kernel, n_riders=len(riders)),
        out_shape=[jax.ShapeDtypeStruct((t, width), BF16)] + r_shapes,
        grid=(batch, groups),
        in_specs=[pl.BlockSpec((seq, gd), lambda b, g: (b, g)),
                  pl.BlockSpec((gd, 2 * gd), lambda b, g: (0, 0)),
                  pl.BlockSpec((seq, 2 * seq), lambda b, g: (0, 0),
                               pipeline_mode=pl.Buffered(1))] + r_in,
        out_specs=[pl.BlockSpec((seq, gd), lambda b, g: (b, g))] + r_out,
        scratch_shapes=[pltpu.VMEM((2 * seq, gd), BF16)],
        compiler_params=_params(("arbitrary", "arbitrary")),
        name="fourier_mix",
    )(f, chan, seqm, *riders)
    return outs[0], outs[1:]


def _attn_kernel(*refs, lam_init, tq, n_riders):
    lam_ref, g_ref, q_ref, k_ref, v_ref = refs[:5]
    o_ref = refs[5 + n_riders]
    s0, s1, p0, p1, l0, l1 = refs[6 + 2 * n_riders:]
    _cast_riders(refs[5:5 + n_riders], refs[6 + n_riders:6 + 2 * n_riders])
    nq = q_ref.shape[0] // tq
    s_buf, p_buf, l_buf = (s0, s1), (p0, p1), (l0, l1)
    lp = lam_ref[...]
    lam = (jnp.exp(jnp.sum(lp[0:1] * lp[1:2], axis=-1, keepdims=True))
           - jnp.exp(jnp.sum(lp[2:3] * lp[3:4], axis=-1, keepdims=True)) + lam_init)

    def scores(j, slot):
        q = q_ref[j * tq:(j + 1) * tq, :]
        for c in range(2):
            sl = slice(c * HEAD_DIM, (c + 1) * HEAD_DIM)
            s_buf[slot][c] = lax.dot_general(q[:, sl], k_ref[:, sl], (((1,), (1,)), ((), ())),
                                             preferred_element_type=F32)

    def softmax(slot):
        for c in range(2):
            for r in range(0, tq, BF16_SUBLANES):
                rows = slice(r, r + BF16_SUBLANES)
                s = s_buf[slot][c, rows, :]
                p = jnp.exp2(s - jnp.max(s, axis=-1, keepdims=True))
                l_buf[slot][c, rows, :] = jnp.broadcast_to(jnp.sum(p, axis=-1, keepdims=True),
                                                           (BF16_SUBLANES, LANES))
                p_buf[slot][c, rows, :] = p.astype(BF16)

    def values(j, slot):
        o = [jnp.dot(p_buf[slot][c], v_ref[...], preferred_element_type=F32)
             / jnp.concatenate([l_buf[slot][c]] * (v_ref.shape[1] // LANES), axis=1)
             for c in range(2)]
        o = o[0] - lam * o[1]
        ms = jnp.mean(o * o, axis=-1, keepdims=True)
        o = o * lax.rsqrt(ms + RMS_EPS) * g_ref[...] * (1.0 - lam_init)
        o_ref[j * tq:(j + 1) * tq, :] = o.astype(o_ref.dtype)

    scores(0, 0)
    for j in range(nq):
        if j + 1 < nq:
            scores(j + 1, (j + 1) % 2)
        softmax(j % 2)
        if j >= 1:
            values(j - 1, (j - 1) % 2)
    values(nq - 1, (nq - 1) % 2)


def _diff_attention(qk, v, lam_params, sub_g, batch, seq, lam_init, riders, tq=256):
    t = qk.shape[0]
    hw = 2 * HEAD_DIM
    r_in, r_shapes, r_out = _rider_specs(riders, batch * HEADS, HEADS)
    outs = pl.pallas_call(
        functools.partial(_attn_kernel, lam_init=lam_init, tq=tq, n_riders=len(riders)),
        out_shape=[jax.ShapeDtypeStruct((t, ATTN_WIDTH), BF16)] + r_shapes,
        grid=(batch, HEADS),
        in_specs=[pl.BlockSpec((4, HEAD_DIM), lambda b, h: (0, 0)),
                  pl.BlockSpec((1, hw), lambda b, h: (0, 0)),
                  pl.BlockSpec((seq, hw), lambda b, h: (b, h)),
                  pl.BlockSpec((seq, hw), lambda b, h: (b, HEADS + h)),
                  pl.BlockSpec((seq, hw), lambda b, h: (b, h))] + r_in,
        out_specs=[pl.BlockSpec((seq, hw), lambda b, h: (b, h))] + r_out,
        scratch_shapes=[pltpu.VMEM((2, tq, seq), F32), pltpu.VMEM((2, tq, seq), F32),
                        pltpu.VMEM((2, tq, seq), BF16), pltpu.VMEM((2, tq, seq), BF16),
                        pltpu.VMEM((2, tq, LANES), F32), pltpu.VMEM((2, tq, LANES), F32)],
        compiler_params=_params(("arbitrary", "arbitrary")),
        name="diff_attention",
    )(lam_params, sub_g, qk, qk, v, *riders)
    return outs[0], outs[1:]


def _merge_kernel(yf_ref, o_ref_in, wf_ref, wa_ref, gf_ref, ga_ref, out_ref):
    y_f = jnp.dot(yf_ref[...], wf_ref[...], preferred_element_type=F32)
    y_a = jnp.dot(o_ref_in[...], wa_ref[...], preferred_element_type=F32)
    out_ref[...] = (gf_ref[...].astype(F32) * y_f + ga_ref[...].astype(F32) * y_a).astype(out_ref.dtype)


def _merge(yf, oa, wf, wa, gates, bm=1024, bn=512):
    t, kf = yf.shape
    ka = oa.shape[1]
    d = wf.shape[1]
    ga_off = d // bn
    return pl.pallas_call(
        _merge_kernel,
        out_shape=jax.ShapeDtypeStruct((t, d), BF16),
        grid=(t // bm, d // bn),
        in_specs=[pl.BlockSpec((bm, kf), lambda i, j: (i, 0)),
                  pl.BlockSpec((bm, ka), lambda i, j: (i, 0)),
                  pl.BlockSpec((kf, bn), lambda i, j: (0, j)),
                  pl.BlockSpec((ka, bn), lambda i, j: (0, j)),
                  pl.BlockSpec((bm, bn), lambda i, j: (i, j)),
                  pl.BlockSpec((bm, bn), lambda i, j: (i, j + ga_off))],
        out_specs=pl.BlockSpec((bm, bn), lambda i, j: (i, j)),
        compiler_params=_params(("arbitrary", "arbitrary")),
        name="merge_proj",
    )(yf, oa, wf, wa, gates, gates)


def _mm_residual_kernel(a_ref, w_ref, r_ref, o_ref):
    acc = jnp.dot(a_ref[...], w_ref[...], preferred_element_type=F32)
    o_ref[...] = r_ref[...] + acc


def _mm_residual(a, w, res, bm=512, bn=512):
    t, k = a.shape
    n = w.shape[1]
    return pl.pallas_call(
        _mm_residual_kernel,
        out_shape=jax.ShapeDtypeStruct((t, n), F32),
        grid=(t // bm, n // bn),
        in_specs=[pl.BlockSpec((bm, k), lambda i, j: (i, 0)),
                  pl.BlockSpec((k, bn), lambda i, j: (0, j)),
                  pl.BlockSpec((bm, bn), lambda i, j: (i, j))],
        out_specs=pl.BlockSpec((bm, bn), lambda i, j: (i, j)),
        compiler_params=_params(("arbitrary", "arbitrary")),
        name="mm_residual",
    )(a, w, res)


def _mm_residual_stats_kernel(a_ref, w_ref, r_ref, g_ref, h_ref, hg_ref, ssq_ref):
    h = r_ref[...] + jnp.dot(a_ref[...], w_ref[...], preferred_element_type=F32)
    h_ref[...] = h
    hg_ref[...] = (h * g_ref[...]).astype(hg_ref.dtype)

    @pl.when(pl.program_id(1) == 0)
    def _():
        ssq_ref[...] = jnp.zeros_like(ssq_ref)

    ssq_ref[...] += jnp.sum(h * h, axis=-1, keepdims=True)


def _mm_residual_stats(a, w, res, g_row, bm=1024, bn=512):
    t, k = a.shape
    n = w.shape[1]
    return pl.pallas_call(
        _mm_residual_stats_kernel,
        out_shape=(jax.ShapeDtypeStruct((t, n), F32),
                   jax.ShapeDtypeStruct((t, n), BF16),
                   jax.ShapeDtypeStruct((t, 1), F32)),
        grid=(t // bm, n // bn),
        in_specs=[pl.BlockSpec((bm, k), lambda i, j: (i, 0)),
                  pl.BlockSpec((k, bn), lambda i, j: (0, j)),
                  pl.BlockSpec((bm, bn), lambda i, j: (i, j)),
                  pl.BlockSpec((1, bn), lambda i, j: (0, j))],
        out_specs=(pl.BlockSpec((bm, bn), lambda i, j: (i, j)),
                   pl.BlockSpec((bm, bn), lambda i, j: (i, j)),
                   pl.BlockSpec((bm, 1), lambda i, j: (i, 0))),
        compiler_params=_params(("arbitrary", "arbitrary")),
        name="mm_residual_stats",
    )(a, w, res, g_row)


def _swiglu_kernel(hg_ref, ssq_ref, wg_ref, wu_ref, o_ref):
    hg = hg_ref[...]
    rstd = lax.rsqrt(ssq_ref[...] * (1.0 / hg.shape[1]) + RMS_EPS)
    gate = jnp.dot(hg, wg_ref[...], preferred_element_type=F32) * rstd
    up = jnp.dot(hg, wu_ref[...], preferred_element_type=F32) * rstd
    o_ref[...] = (gate * _sigmoid(gate) * up).astype(o_ref.dtype)


def _swiglu(hg, ssq, wg, wu, bm=1024, bn=256):
    t, k = hg.shape
    n = wg.shape[1]
    return pl.pallas_call(
        _swiglu_kernel,
        out_shape=jax.ShapeDtypeStruct((t, n), BF16),
        grid=(t // bm, n // bn),
        in_specs=[pl.BlockSpec((bm, k), lambda i, j: (i, 0)),
                  pl.BlockSpec((bm, 1), lambda i, j: (i, 0)),
                  pl.BlockSpec((k, bn), lambda i, j: (0, j)),
                  pl.BlockSpec((k, bn), lambda i, j: (0, j))],
        out_specs=pl.BlockSpec((bm, bn), lambda i, j: (i, j)),
        compiler_params=_params(("arbitrary", "arbitrary")),
        name="swiglu_up",
    )(hg, ssq, wg, wu)


def kernel(x, positions, norm_mix_g, w_in, b_gate, lambda_q1, lambda_k1, lambda_q2, lambda_k2,
           subln_g, w_fourier_out, w_attn_out, w_out, norm_ffn_g, w_ffn_gate, w_ffn_up,
           w_ffn_down, norm_final_g):
    batch, seq, d = x.shape
    depth = w_in.shape[0]
    t = batch * seq
    h = x.reshape(t, d)

    inv_freq = ROPE_THETA ** (-jnp.arange(0, HEAD_DIM, 2, dtype=F32) / HEAD_DIM)
    invf_full = jnp.concatenate([inv_freq, inv_freq]).reshape(1, HEAD_DIM)
    pos_col = positions.reshape(t, 1)
    qk_scale = jnp.concatenate([jnp.full((1, ATTN_WIDTH), math.log2(math.e) / math.sqrt(HEAD_DIM), F32),
                                jnp.ones((1, ATTN_WIDTH), F32)], axis=1)
    chan, seqm = _dft_constants(seq)

    for l in range(depth):
        lam_init = 0.8 - 0.6 * math.exp(-0.3 * l)
        w_in_b = w_in[l].astype(BF16)
        if l == 0:
            u, cos_t, sin_t = _rmsnorm(h, norm_mix_g[l].reshape(1, d), BF16, rope=(pos_col, invf_full))
        else:
            u = _rmsnorm(h, norm_mix_g[l].reshape(1, d), BF16)
        f = _in_proj(u, w_in_b, 0, FOURIER_WIDTH, "cast")
        qk = _in_proj(u, w_in_b, FOURIER_WIDTH, 2 * ATTN_WIDTH, "rope", (cos_t, sin_t, qk_scale))
        v = _in_proj(u, w_in_b, FOURIER_WIDTH + 2 * ATTN_WIDTH, ATTN_WIDTH, "cast")
        gates = _in_proj(u, w_in_b, FOURIER_WIDTH + 3 * ATTN_WIDTH, 2 * d, "sigmoid",
                         (b_gate[l].reshape(1, 2 * d),))

        y_f, (w_f_b, w_a_b, w_out_b, w_down_b) = _fourier_mix(
            f, batch, seq, chan, seqm, [w_fourier_out[l], w_attn_out[l], w_out[l], w_ffn_down[l]])
        lam_params = jnp.stack([lambda_q1[l], lambda_k1[l], lambda_q2[l], lambda_k2[l]]).astype(F32)
        o_a, (w_gate_b, w_up_b) = _diff_attention(
            qk, v, lam_params, subln_g[l].reshape(1, 2 * HEAD_DIM).astype(F32), batch, seq, lam_init,
            [w_ffn_gate[l], w_ffn_up[l]])

        merged = _merge(y_f, o_a, w_f_b, w_a_b, gates)
        h, hg, ssq = _mm_residual_stats(merged, w_out_b, h, norm_ffn_g[l].reshape(1, d).astype(F32))
        hid = _swiglu(hg, ssq, w_gate_b, w_up_b)
        h = _mm_residual(hid, w_down_b, h)

    out = _rmsnorm(h, norm_final_g.reshape(1, d), x.dtype)
    return out.reshape(batch, seq, d)
```

```python
import functools
import math

import numpy as np
import jax
import jax.numpy as jnp
from jax import lax
from jax.experimental import pallas as pl
from jax.experimental.pallas import tpu as pltpu

F32 = jnp.float32
BF16 = jnp.bfloat16

D_MODEL = 4096
FOURIER_WIDTH = D_MODEL // 2
FOURIER_GROUPS = 8
FOURIER_GROUP_DIM = FOURIER_WIDTH // FOURIER_GROUPS
HEAD_DIM = 128
HEADS = (D_MODEL // 2) // (2 * HEAD_DIM)
ATTN_WIDTH = HEADS * 2 * HEAD_DIM
FFN_HIDDEN = -(-8 * D_MODEL // (3 * 256)) * 256
ROPE_THETA = 10000.0
RMS_EPS = 1e-6

LANES = 128
BF16_SUBLANES = 16
VMEM_LIMIT = 56 * 1024 * 1024


def _params(sem):
    return pltpu.CompilerParams(dimension_semantics=sem, vmem_limit_bytes=VMEM_LIMIT)


def _sigmoid(x):
    return 0.5 * jnp.tanh(0.5 * x) + 0.5


def _rider_rows(rows, steps):
    for nblk in range(steps, 0, -1):
        if rows % nblk == 0 and (rows // nblk) % BF16_SUBLANES == 0:
            return rows // nblk
    raise ValueError(f"cannot split {rows} rows into at most {steps} bf16-tileable blocks")


def _rider_specs(weights, steps, inner):
    in_specs, out_shapes, out_specs = [], [], []
    for w in weights:
        rb = _rider_rows(w.shape[0], steps)
        last = w.shape[0] // rb - 1
        index = lambda a, b, last=last: (jnp.minimum(a * inner + b, last), 0)
        in_specs.append(pl.BlockSpec((rb, w.shape[1]), index))
        out_specs.append(pl.BlockSpec((rb, w.shape[1]), index))
        out_shapes.append(jax.ShapeDtypeStruct(w.shape, BF16))
    return in_specs, out_shapes, out_specs


def _cast_riders(rider_in, rider_out):
    for w_ref, o_ref in zip(rider_in, rider_out):
        o_ref[...] = w_ref[...].astype(o_ref.dtype)


def _rope_tables(pos_ref, invf_ref, cos_ref, sin_ref):
    ang = pos_ref[...].astype(F32) * invf_ref[...]
    lane = lax.broadcasted_iota(jnp.int32, ang.shape, 1)
    s = jnp.sin(ang)
    cos_ref[...] = jnp.cos(ang)
    sin_ref[...] = jnp.where(lane < HEAD_DIM // 2, -s, s)


def _rmsnorm_kernel(x_ref, g_ref, o_ref):
    x = x_ref[...].astype(F32)
    ms = jnp.mean(x * x, axis=-1, keepdims=True)
    o_ref[...] = (x * lax.rsqrt(ms + RMS_EPS) * g_ref[...]).astype(o_ref.dtype)


def _rmsnorm_rope_kernel(x_ref, g_ref, pos_ref, invf_ref, o_ref, cos_ref, sin_ref):
    _rmsnorm_kernel(x_ref, g_ref, o_ref)
    _rope_tables(pos_ref, invf_ref, cos_ref, sin_ref)


def _rmsnorm(x, g_row, out_dtype, rope=None, tm=256):
    t, d = x.shape
    row = lambda i: (i, 0)
    fixed = lambda i: (0, 0)
    in_specs = [pl.BlockSpec((tm, d), row), pl.BlockSpec((1, d), fixed)]
    out_shape = jax.ShapeDtypeStruct((t, d), out_dtype)
    out_specs = pl.BlockSpec((tm, d), row)
    if rope is None:
        body, args = _rmsnorm_kernel, (x, g_row)
    else:
        body, args = _rmsnorm_rope_kernel, (x, g_row, *rope)
        in_specs += [pl.BlockSpec((tm, 1), row), pl.BlockSpec((1, HEAD_DIM), fixed)]
        out_shape = (out_shape,) + (jax.ShapeDtypeStruct((t, HEAD_DIM), F32),) * 2
        out_specs = (out_specs,) + (pl.BlockSpec((tm, HEAD_DIM), row),) * 2
    return pl.pallas_call(
        body,
        out_shape=out_shape,
        grid=(t // tm,),
        in_specs=in_specs,
        out_specs=out_specs,
        compiler_params=_params(("arbitrary",)),
        name="rmsnorm",
    )(*args)


def _mm_cast_kernel(a_ref, w_ref, o_ref):
    o_ref[...] = jnp.dot(a_ref[...], w_ref[...], preferred_element_type=F32).astype(o_ref.dtype)


def _mm_rope_kernel(a_ref, w_ref, cos_ref, sin_ref, cs_ref, o_ref):
    acc = jnp.dot(a_ref[...], w_ref[...], preferred_element_type=F32)
    cos = cos_ref[...]
    sin = sin_ref[...]
    for c in range(acc.shape[1] // HEAD_DIM):
        sl = slice(c * HEAD_DIM, (c + 1) * HEAD_DIM)
        xc = acc[:, sl]
        r = xc * cos + pltpu.roll(xc, HEAD_DIM // 2, 1) * sin
        o_ref[:, sl] = (r * cs_ref[:, sl]).astype(o_ref.dtype)


def _mm_sigmoid_kernel(a_ref, w_ref, b_ref, o_ref):
    acc = jnp.dot(a_ref[...], w_ref[...], preferred_element_type=F32)
    o_ref[...] = _sigmoid(acc + b_ref[...]).astype(o_ref.dtype)


def _in_proj(u, w, col0, ncols, kind, extras=(), bm=1024, bn=1024):
    t, k = u.shape
    joff = col0 // bn
    in_specs = [pl.BlockSpec((bm, k), lambda i, j: (i, 0)),
                pl.BlockSpec((k, bn), lambda i, j: (0, j + joff))]
    if kind == "cast":
        body = _mm_cast_kernel
    elif kind == "rope":
        body = _mm_rope_kernel
        in_specs += [pl.BlockSpec((bm, HEAD_DIM), lambda i, j: (i, 0)),
                     pl.BlockSpec((bm, HEAD_DIM), lambda i, j: (i, 0)),
                     pl.BlockSpec((1, bn), lambda i, j: (0, j))]
    else:
        body = _mm_sigmoid_kernel
        in_specs += [pl.BlockSpec((1, bn), lambda i, j: (0, j))]
    return pl.pallas_call(
        body,
        out_shape=jax.ShapeDtypeStruct((t, ncols), BF16),
        grid=(t // bm, ncols // bn),
        in_specs=in_specs,
        out_specs=pl.BlockSpec((bm, bn), lambda i, j: (i, j)),
        compiler_params=_params(("arbitrary", "arbitrary")),
        name="in_proj_" + kind,
    )(u, w, *extras)


def _dft_constants(seq):
    gd = FOURIER_GROUP_DIM
    jc = np.arange(gd, dtype=np.int64)
    ang_c = 2.0 * np.pi * ((jc[:, None] * jc[None, :]) % gd) / gd
    chan = np.concatenate([np.cos(ang_c), np.sin(ang_c)], axis=1) / math.sqrt(gd)
    js = np.arange(seq, dtype=np.int64)
    ang_s = 2.0 * np.pi * ((js[:, None] * js[None, :]) % seq) / seq
    seqm = np.concatenate([np.cos(ang_s), -np.sin(ang_s)], axis=1) / math.sqrt(seq)
    return jnp.asarray(chan, dtype=BF16), jnp.asarray(seqm, dtype=BF16)


def _fourier_kernel(*refs, n_riders):
    f_ref, chan_ref, seqm_ref = refs[:3]
    o_ref, pq_ref = refs[3 + n_riders], refs[-1]
    _cast_riders(refs[3:3 + n_riders], refs[4 + n_riders:4 + 2 * n_riders])
    s = f_ref.shape[0]
    gd = FOURIER_GROUP_DIM
    pq = jnp.dot(f_ref[...], chan_ref[...], preferred_element_type=F32)
    pq_ref[0:s, :] = pq[:, 0:gd].astype(BF16)
    pq_ref[s:2 * s, :] = pq[:, gd:2 * gd].astype(BF16)
    o_ref[...] = jnp.dot(seqm_ref[...], pq_ref[...], preferred_element_type=F32).astype(o_ref.dtype)


def _fourier_mix(f, batch, seq, chan, seqm, riders):
    t, width = f.shape
    gd = FOURIER_GROUP_DIM
    groups = width // gd
    r_in, r_shapes, r_out = _rider_specs(riders, batch * groups, groups)
    outs = pl.pallas_call(
        functools.partial(_fourier_kernel, n_riders=len(riders)),
        out_shape=[jax.ShapeDtypeStruct((t, width), BF16)] + r_shapes,
        grid=(batch, groups),
        in_specs=[pl.BlockSpec((seq, gd), lambda b, g: (b, g)),
                  pl.BlockSpec((gd, 2 * gd), lambda b, g: (0, 0)),
                  pl.BlockSpec((seq, 2 * seq), lambda b, g: (0, 0),
                               pipeline_mode=pl.Buffered(1))] + r_in,
        out_specs=[pl.BlockSpec((seq, gd), lambda b, g: (b, g))] + r_out,
        scratch_shapes=[pltpu.VMEM((2 * seq, gd), BF16)],
        compiler_params=_params(("arbitrary", "arbitrary")),
        name="fourier_mix",
    )(f, chan, seqm, *riders)
    return outs[0], outs[1:]


def _attn_kernel(*refs, lam_init, tq, n_riders):
    lam_ref, g_ref, q_ref, k_ref, v_ref = refs[:5]
    o_ref = refs[5 + n_riders]
    s0, s1, p0, p1, l0, l1 = refs[6 + 2 * n_riders:]
    _cast_riders(refs[5:5 + n_riders], refs[6 + n_riders:6 + 2 * n_riders])
    nq = q_ref.shape[0] // tq
    s_buf, p_buf, l_buf = (s0, s1), (p0, p1), (l0, l1)
    lp = lam_ref[...]
    lam = (jnp.exp(jnp.sum(lp[0:1] * lp[1:2], axis=-1, keepdims=True))
           - jnp.exp(jnp.sum(lp[2:3] * lp[3:4], axis=-1, keepdims=True)) + lam_init)

    def scores(j, slot):
        q = q_ref[j * tq:(j + 1) * tq, :]
        for c in range(2):
            sl = slice(c * HEAD_DIM, (c + 1) * HEAD_DIM)
            s_buf[slot][c] = lax.dot_general(q[:, sl], k_ref[:, sl], (((1,), (1,)), ((), ())),
                                             preferred_element_type=F32)

    def softmax(slot):
        for c in range(2):
            for r in range(0, tq, BF16_SUBLANES):
                rows = slice(r, r + BF16_SUBLANES)
                s = s_buf[slot][c, rows, :]
                p = jnp.exp2(s - jnp.max(s, axis=-1, keepdims=True))
                l_buf[slot][c, rows, :] = jnp.broadcast_to(jnp.sum(p, axis=-1, keepdims=True),
                                                           (BF16_SUBLANES, LANES))
                p_buf[slot][c, rows, :] = p.astype(BF16)

    def values(j, slot):
        o = [jnp.dot(p_buf[slot][c], v_ref[...], preferred_element_type=F32)
             / jnp.concatenate([l_buf[slot][c]] * (v_ref.shape[1] // LANES), axis=1)
             for c in range(2)]
        o = o[0] - lam * o[1]
        ms = jnp.mean(o * o, axis=-1, keepdims=True)
        o = o * lax.rsqrt(ms + RMS_EPS) * g_ref[...] * (1.0 - lam_init)
        o_ref[j * tq:(j + 1) * tq, :] = o.astype(o_ref.dtype)

    scores(0, 0)
    for j in range(nq):
        if j + 1 < nq:
            scores(j + 1, (j + 1) % 2)
        softmax(j % 2)
        if j >= 1:
            values(j - 1, (j - 1) % 2)
    values(nq - 1, (nq - 1) % 2)


def _diff_attention(qk, v, lam_params, sub_g, batch, seq, lam_init, riders, tq=256):
    t = qk.shape[0]
    hw = 2 * HEAD_DIM
    r_in, r_shapes, r_out = _rider_specs(riders, batch * HEADS, HEADS)
    outs = pl.pallas_call(
        functools.partial(_attn_kernel, lam_init=lam_init, tq=tq, n_riders=len(riders)),
        out_shape=[jax.ShapeDtypeStruct((t, ATTN_WIDTH), BF16)] + r_shapes,
        grid=(batch, HEADS),
        in_specs=[pl.BlockSpec((4, HEAD_DIM), lambda b, h: (0, 0)),
                  pl.BlockSpec((1, hw), lambda b, h: (0, 0)),
                  pl.BlockSpec((seq, hw), lambda b, h: (b, h)),
                  pl.BlockSpec((seq, hw), lambda b, h: (b, HEADS + h)),
                  pl.BlockSpec((seq, hw), lambda b, h: (b, h))] + r_in,
        out_specs=[pl.BlockSpec((seq, hw), lambda b, h: (b, h))] + r_out,
        scratch_shapes=[pltpu.VMEM((2, tq, seq), F32), pltpu.VMEM((2, tq, seq), F32),
                        pltpu.VMEM((2, tq, seq), BF16), pltpu.VMEM((2, tq, seq), BF16),
                        pltpu.VMEM((2, tq, LANES), F32), pltpu.VMEM((2, tq, LANES), F32)],
        compiler_params=_params(("arbitrary", "arbitrary")),
        name="diff_attention",
    )(lam_params, sub_g, qk, qk, v, *riders)
    return outs[0], outs[1:]


def _merge_kernel(yf_ref, o_ref_in, wf_ref, wa_ref, gf_ref, ga_ref, out_ref):
    y_f = jnp.dot(yf_ref[...], wf_ref[...], preferred_element_type=F32)
    y_a = jnp.dot(o_ref_in[...], wa_ref[...], preferred_element_type=F32)
    out_ref[...] = (gf_ref[...].astype(F32) * y_f + ga_ref[...].astype(F32) * y_a).astype(out_ref.dtype)


def _merge(yf, oa, wf, wa, gates, bm=1024, bn=512):
    t, kf = yf.shape
    ka = oa.shape[1]
    d = wf.shape[1]
    ga_off = d // bn
    return pl.pallas_call(
        _merge_kernel,
        out_shape=jax.ShapeDtypeStruct((t, d), BF16),
        grid=(t // bm, d // bn),
        in_specs=[pl.BlockSpec((bm, kf), lambda i, j: (i, 0)),
                  pl.BlockSpec((bm, ka), lambda i, j: (i, 0)),
                  pl.BlockSpec((kf, bn), lambda i, j: (0, j)),
                  pl.BlockSpec((ka, bn), lambda i, j: (0, j)),
                  pl.BlockSpec((bm, bn), lambda i, j: (i, j)),
                  pl.BlockSpec((bm, bn), lambda i, j: (i, j + ga_off))],
        out_specs=pl.BlockSpec((bm, bn), lambda i, j: (i, j)),
        compiler_params=_params(("arbitrary", "arbitrary")),
        name="merge_proj",
    )(yf, oa, wf, wa, gates, gates)


def _mm_residual_kernel(a_ref, w_ref, r_ref, o_ref):
    acc = jnp.dot(a_ref[...], w_ref[...], preferred_element_type=F32)
    o_ref[...] = r_ref[...] + acc


def _mm_residual(a, w, res, bm=512, bn=512):
    t, k = a.shape
    n = w.shape[1]
    return pl.pallas_call(
        _mm_residual_kernel,
        out_shape=jax.ShapeDtypeStruct((t, n), F32),
        grid=(t // bm, n // bn),
        in_specs=[pl.BlockSpec((bm, k), lambda i, j: (i, 0)),
                  pl.BlockSpec((k, bn), lambda i, j: (0, j)),
                  pl.BlockSpec((bm, bn), lambda i, j: (i, j))],
        out_specs=pl.BlockSpec((bm, bn), lambda i, j: (i, j)),
        compiler_params=_params(("arbitrary", "arbitrary")),
        name="mm_residual",
    )(a, w, res)


def _lane_partial_sumsq(h):
    hh = h * h
    return sum(hh[:, c * LANES:(c + 1) * LANES] for c in range(h.shape[1] // LANES))


def _mm_residual_norm_kernel(a_ref, w_ref, r_ref, g_ref, o_ref, ssq_ref):
    j = pl.program_id(1)
    bn = w_ref.shape[1]
    h = r_ref[...] + jnp.dot(a_ref[...], w_ref[...], preferred_element_type=F32)
    o_ref[:, pl.ds(pl.multiple_of(j * bn, bn), bn)] = h

    @pl.when(j == 0)
    def _():
        ssq_ref[...] = jnp.zeros_like(ssq_ref)

    ssq_ref[...] += _lane_partial_sumsq(h)

    @pl.when(j == pl.num_programs(1) - 1)
    def _():
        ms = jnp.sum(ssq_ref[...], axis=-1, keepdims=True) * (1.0 / o_ref.shape[1])
        rstd = lax.rsqrt(ms + RMS_EPS)
        for c in range(o_ref.shape[1] // bn):
            sl = slice(c * bn, (c + 1) * bn)
            o_ref[:, sl] = o_ref[:, sl] * rstd * g_ref[:, sl]


def _mm_residual_norm(a, w, res, g_row, bm=512, bn=256):
    t, k = a.shape
    n = w.shape[1]
    return pl.pallas_call(
        _mm_residual_norm_kernel,
        out_shape=jax.ShapeDtypeStruct((t, n), F32),
        grid=(t // bm, n // bn),
        in_specs=[pl.BlockSpec((bm, k), lambda i, j: (i, 0)),
                  pl.BlockSpec((k, bn), lambda i, j: (0, j)),
                  pl.BlockSpec((bm, bn), lambda i, j: (i, j)),
                  pl.BlockSpec((1, n), lambda i, j: (0, 0))],
        out_specs=pl.BlockSpec((bm, n), lambda i, j: (i, 0)),
        scratch_shapes=[pltpu.VMEM((bm, LANES), F32)],
        compiler_params=_params(("arbitrary", "arbitrary")),
        name="mm_residual_norm",
    )(a, w, res, g_row)


def _mm_residual_stats_kernel(a_ref, w_ref, r_ref, g_ref, h_ref, hg_ref, ssq_ref):
    h = r_ref[...] + jnp.dot(a_ref[...], w_ref[...], preferred_element_type=F32)
    h_ref[...] = h
    hg_ref[...] = (h * g_ref[...]).astype(hg_ref.dtype)

    @pl.when(pl.program_id(1) == 0)
    def _():
        ssq_ref[...] = jnp.zeros_like(ssq_ref)

    ssq_ref[...] += jnp.sum(h * h, axis=-1, keepdims=True)


def _mm_residual_stats(a, w, res, g_row, bm=1024, bn=512):
    t, k = a.shape
    n = w.shape[1]
    return pl.pallas_call(
        _mm_residual_stats_kernel,
        out_shape=(jax.ShapeDtypeStruct((t, n), F32),
                   jax.ShapeDtypeStruct((t, n), BF16),
                   jax.ShapeDtypeStruct((t, 1), F32)),
        grid=(t // bm, n // bn),
        in_specs=[pl.BlockSpec((bm, k), lambda i, j: (i, 0)),
                  pl.BlockSpec((k, bn), lambda i, j: (0, j)),
                  pl.BlockSpec((bm, bn), lambda i, j: (i, j)),
                  pl.BlockSpec((1, bn), lambda i, j: (0, j))],
        out_specs=(pl.BlockSpec((bm, bn), lambda i, j: (i, j)),
                   pl.BlockSpec((bm, bn), lambda i, j: (i, j)),
                   pl.BlockSpec((bm, 1), lambda i, j: (i, 0))),
        compiler_params=_params(("arbitrary", "arbitrary")),
        name="mm_residual_stats",
    )(a, w, res, g_row)


def _swiglu_kernel(hg_ref, ssq_ref, wg_ref, wu_ref, o_ref):
    hg = hg_ref[...]
    rstd = lax.rsqrt(ssq_ref[...] * (1.0 / hg.shape[1]) + RMS_EPS)
    gate = jnp.dot(hg, wg_ref[...], preferred_element_type=F32) * rstd
    up = jnp.dot(hg, wu_ref[...], preferred_element_type=F32) * rstd
    o_ref[...] = (gate * _sigmoid(gate) * up).astype(o_ref.dtype)


def _swiglu(hg, ssq, wg, wu, bm=1024, bn=256):
    t, k = hg.shape
    n = wg.shape[1]
    return pl.pallas_call(
        _swiglu_kernel,
        out_shape=jax.ShapeDtypeStruct((t, n), BF16),
        grid=(t // bm, n // bn),
        in_specs=[pl.BlockSpec((bm, k), lambda i, j: (i, 0)),
                  pl.BlockSpec((bm, 1), lambda i, j: (i, 0)),
                  pl.BlockSpec((k, bn), lambda i, j: (0, j)),
                  pl.BlockSpec((k, bn), lambda i, j: (0, j))],
        out_specs=pl.BlockSpec((bm, bn), lambda i, j: (i, j)),
        compiler_params=_params(("arbitrary", "arbitrary")),
        name="swiglu_up",
    )(hg, ssq, wg, wu)


def kernel(x, positions, norm_mix_g, w_in, b_gate, lambda_q1, lambda_k1, lambda_q2, lambda_k2,
           subln_g, w_fourier_out, w_attn_out, w_out, norm_ffn_g, w_ffn_gate, w_ffn_up,
           w_ffn_down, norm_final_g):
    batch, seq, d = x.shape
    depth = w_in.shape[0]
    t = batch * seq
    h = x.reshape(t, d)

    inv_freq = ROPE_THETA ** (-jnp.arange(0, HEAD_DIM, 2, dtype=F32) / HEAD_DIM)
    invf_full = jnp.concatenate([inv_freq, inv_freq]).reshape(1, HEAD_DIM)
    pos_col = positions.reshape(t, 1)
    qk_scale = jnp.concatenate([jnp.full((1, ATTN_WIDTH), math.log2(math.e) / math.sqrt(HEAD_DIM), F32),
                                jnp.ones((1, ATTN_WIDTH), F32)], axis=1)
    chan, seqm = _dft_constants(seq)

    for l in range(depth):
        lam_init = 0.8 - 0.6 * math.exp(-0.3 * l)
        w_in_b = w_in[l].astype(BF16)
        if l == 0:
            u, cos_t, sin_t = _rmsnorm(h, norm_mix_g[l].reshape(1, d), BF16, rope=(pos_col, invf_full))
        else:
            u = _rmsnorm(h, norm_mix_g[l].reshape(1, d), BF16)
        f = _in_proj(u, w_in_b, 0, FOURIER_WIDTH, "cast")
        qk = _in_proj(u, w_in_b, FOURIER_WIDTH, 2 * ATTN_WIDTH, "rope", (cos_t, sin_t, qk_scale))
        v = _in_proj(u, w_in_b, FOURIER_WIDTH + 2 * ATTN_WIDTH, ATTN_WIDTH, "cast")
        gates = _in_proj(u, w_in_b, FOURIER_WIDTH + 3 * ATTN_WIDTH, 2 * d, "sigmoid",
                         (b_gate[l].reshape(1, 2 * d),))

        y_f, (w_f_b, w_a_b, w_out_b, w_down_b) = _fourier_mix(
            f, batch, seq, chan, seqm, [w_fourier_out[l], w_attn_out[l], w_out[l], w_ffn_down[l]])
        lam_params = jnp.stack([lambda_q1[l], lambda_k1[l], lambda_q2[l], lambda_k2[l]]).astype(F32)
        o_a, (w_gate_b, w_up_b) = _diff_attention(
            qk, v, lam_params, subln_g[l].reshape(1, 2 * HEAD_DIM).astype(F32), batch, seq, lam_init,
            [w_ffn_gate[l], w_ffn_up[l]])

        merged = _merge(y_f, o_a, w_f_b, w_a_b, gates)
        h, hg, ssq = _mm_residual_stats(merged, w_out_b, h, norm_ffn_g[l].reshape(1, d).astype(F32))
        hid = _swiglu(hg, ssq, w_gate_b, w_up_b)
        if l + 1 < depth:
            h = _mm_residual(hid, w_down_b, h)
        else:
            h = _mm_residual_norm(hid, w_down_b, h, norm_final_g.reshape(1, d).astype(F32))

    return h.astype(x.dtype).reshape(batch, seq, d)
```

```python
import functools
import math

import numpy as np
import jax
import jax.numpy as jnp
from jax import lax
from jax.experimental import pallas as pl
from jax.experimental.pallas import tpu as pltpu

F32 = jnp.float32
BF16 = jnp.bfloat16

D_MODEL = 4096
FOURIER_WIDTH = D_MODEL // 2
FOURIER_GROUPS = 8
FOURIER_GROUP_DIM = FOURIER_WIDTH // FOURIER_GROUPS
HEAD_DIM = 128
HEADS = (D_MODEL // 2) // (2 * HEAD_DIM)
ATTN_WIDTH = HEADS * 2 * HEAD_DIM
FFN_HIDDEN = -(-8 * D_MODEL // (3 * 256)) * 256
ROPE_THETA = 10000.0
RMS_EPS = 1e-6

LANES = 128
BF16_SUBLANES = 16
VMEM_LIMIT = 56 * 1024 * 1024


def _params(sem):
    return pltpu.CompilerParams(dimension_semantics=sem, vmem_limit_bytes=VMEM_LIMIT)


def _sigmoid(x):
    return 0.5 * jnp.tanh(0.5 * x) + 0.5


def _rider_rows(rows, steps):
    for nblk in range(steps, 0, -1):
        if rows % nblk == 0 and (rows // nblk) % BF16_SUBLANES == 0:
            return rows // nblk
    raise ValueError(f"cannot split {rows} rows into at most {steps} bf16-tileable blocks")


def _rider_specs(weights, steps, inner):
    in_specs, out_shapes, out_specs = [], [], []
    for w in weights:
        rb = _rider_rows(w.shape[0], steps)
        last = w.shape[0] // rb - 1
        index = lambda a, b, last=last: (jnp.minimum(a * inner + b, last), 0)
        in_specs.append(pl.BlockSpec((rb, w.shape[1]), index))
        out_specs.append(pl.BlockSpec((rb, w.shape[1]), index))
        out_shapes.append(jax.ShapeDtypeStruct(w.shape, BF16))
    return in_specs, out_shapes, out_specs


def _cast_riders(rider_in, rider_out):
    for w_ref, o_ref in zip(rider_in, rider_out):
        o_ref[...] = w_ref[...].astype(o_ref.dtype)


def _rope_tables(pos_ref, invf_ref, cos_ref, sin_ref):
    ang = pos_ref[...].astype(F32) * invf_ref[...]
    lane = lax.broadcasted_iota(jnp.int32, ang.shape, 1)
    s = jnp.sin(ang)
    cos_ref[...] = jnp.cos(ang)
    sin_ref[...] = jnp.where(lane < HEAD_DIM // 2, -s, s)


def _rmsnorm_kernel(x_ref, g_ref, o_ref):
    x = x_ref[...].astype(F32)
    ms = jnp.mean(x * x, axis=-1, keepdims=True)
    o_ref[...] = (x * lax.rsqrt(ms + RMS_EPS) * g_ref[...]).astype(o_ref.dtype)


def _rmsnorm_rope_kernel(x_ref, g_ref, pos_ref, invf_ref, o_ref, cos_ref, sin_ref):
    _rmsnorm_kernel(x_ref, g_ref, o_ref)
    _rope_tables(pos_ref, invf_ref, cos_ref, sin_ref)


def _rmsnorm(x, g_row, out_dtype, rope=None, tm=256):
    t, d = x.shape
    row = lambda i: (i, 0)
    fixed = lambda i: (0, 0)
    in_specs = [pl.BlockSpec((tm, d), row), pl.BlockSpec((1, d), fixed)]
    out_shape = jax.ShapeDtypeStruct((t, d), out_dtype)
    out_specs = pl.BlockSpec((tm, d), row)
    if rope is None:
        body, args = _rmsnorm_kernel, (x, g_row)
    else:
        body, args = _rmsnorm_rope_kernel, (x, g_row, *rope)
        in_specs += [pl.BlockSpec((tm, 1), row), pl.BlockSpec((1, HEAD_DIM), fixed)]
        out_shape = (out_shape,) + (jax.ShapeDtypeStruct((t, HEAD_DIM), F32),) * 2
        out_specs = (out_specs,) + (pl.BlockSpec((tm, HEAD_DIM), row),) * 2
    return pl.pallas_call(
        body,
        out_shape=out_shape,
        grid=(t // tm,),
        in_specs=in_specs,
        out_specs=out_specs,
        compiler_params=_params(("arbitrary",)),
        name="rmsnorm",
    )(*args)


def _mm_cast_kernel(a_ref, w_ref, o_ref):
    o_ref[...] = jnp.dot(a_ref[...], w_ref[...], preferred_element_type=F32).astype(o_ref.dtype)


def _mm_rope_kernel(a_ref, w_ref, cos_ref, sin_ref, cs_ref, o_ref):
    acc = jnp.dot(a_ref[...], w_ref[...], preferred_element_type=F32)
    cos = cos_ref[...]
    sin = sin_ref[...]
    for c in range(acc.shape[1] // HEAD_DIM):
        sl = slice(c * HEAD_DIM, (c + 1) * HEAD_DIM)
        xc = acc[:, sl]
        r = xc * cos + pltpu.roll(xc, HEAD_DIM // 2, 1) * sin
        o_ref[:, sl] = (r * cs_ref[:, sl]).astype(o_ref.dtype)


def _mm_sigmoid_kernel(a_ref, w_ref, b_ref, o_ref):
    acc = jnp.dot(a_ref[...], w_ref[...], preferred_element_type=F32)
    o_ref[...] = _sigmoid(acc + b_ref[...]).astype(o_ref.dtype)


def _in_proj(u, w, col0, ncols, kind, extras=(), bm=1024, bn=1024):
    t, k = u.shape
    joff = col0 // bn
    in_specs = [pl.BlockSpec((bm, k), lambda i, j: (i, 0)),
                pl.BlockSpec((k, bn), lambda i, j: (0, j + joff))]
    if kind == "cast":
        body = _mm_cast_kernel
    elif kind == "rope":
        body = _mm_rope_kernel
        in_specs += [pl.BlockSpec((bm, HEAD_DIM), lambda i, j: (i, 0)),
                     pl.BlockSpec((bm, HEAD_DIM), lambda i, j: (i, 0)),
                     pl.BlockSpec((1, bn), lambda i, j: (0, j))]
    else:
        body = _mm_sigmoid_kernel
        in_specs += [pl.BlockSpec((1, bn), lambda i, j: (0, j))]
    return pl.pallas_call(
        body,
        out_shape=jax.ShapeDtypeStruct((t, ncols), BF16),
        grid=(t // bm, ncols // bn),
        in_specs=in_specs,
        out_specs=pl.BlockSpec((bm, bn), lambda i, j: (i, j)),
        compiler_params=_params(("arbitrary", "arbitrary")),
        name="in_proj_" + kind,
    )(u, w, *extras)


FLIP_TILE = 256


def _dft_constants(seq):
    gd = FOURIER_GROUP_DIM
    half = seq // 2
    jc = np.arange(gd, dtype=np.int64)
    ang_c = 2.0 * np.pi * ((jc[:, None] * jc[None, :]) % gd) / gd
    chan = np.concatenate([np.cos(ang_c), np.sin(ang_c)], axis=1) / math.sqrt(gd)
    j = np.arange(half + BF16_SUBLANES, dtype=np.int64)[:, None]
    n = np.arange(half, dtype=np.int64)[None, :]
    ang_s = 2.0 * np.pi * ((j * n) % seq) / seq
    cos_m = np.cos(ang_s) / math.sqrt(seq)
    cos_m[:, 0] *= 0.5
    cos_m[half + 1:, :] = 0.0
    sin_m = np.sin(ang_s[:half]) / math.sqrt(seq)
    sign = np.where(np.arange(half + BF16_SUBLANES) % 2 == 0, 1.0, -1.0) / math.sqrt(seq)
    sign[half + 1:] = 0.0
    r = np.arange(1, FLIP_TILE)
    exch = np.zeros((FLIP_TILE, FLIP_TILE))
    exch[r, FLIP_TILE - r] = 1.0
    return (jnp.asarray(chan, BF16), jnp.asarray(cos_m, BF16), jnp.asarray(sin_m, BF16),
            jnp.asarray(sign.reshape(-1, 1), F32), jnp.asarray(exch, BF16))


def _flip_tiles(exch, tiles, first_rows):
    row = lax.broadcasted_iota(jnp.int32, tiles[0].shape, 0)
    return [jnp.where(row == 0, first, jnp.dot(exch, tile, preferred_element_type=F32))
            for tile, first in zip(tiles, first_rows)]


def _first_row(x, r):
    return x[r:r + BF16_SUBLANES, :].astype(F32)[0:1, :]


def _fourier_kernel(*refs, n_riders):
    f_ref, chan_ref, cos_ref, sin_ref, sign_ref, exch_ref = refs[:6]
    o_ref = refs[6 + n_riders]
    _cast_riders(refs[6:6 + n_riders], refs[7 + n_riders:7 + 2 * n_riders])
    seq, gd = f_ref.shape
    half = seq // 2
    nt = half // FLIP_TILE
    exch = exch_ref[...]

    rev = jnp.concatenate(_flip_tiles(
        exch,
        [f_ref[seq - (t + 1) * FLIP_TILE:seq - t * FLIP_TILE, :] for t in range(nt)],
        [_first_row(f_ref, 0 if t == 0 else seq - t * FLIP_TILE) for t in range(nt)]), axis=0)
    x_lo = f_ref[0:half, :].astype(F32)
    even = (x_lo + rev).astype(BF16)
    odd = (x_lo - rev).astype(BF16)
    p_even = jnp.dot(even, chan_ref[:, 0:gd], preferred_element_type=F32).astype(BF16)
    q_odd = jnp.dot(odd, chan_ref[:, gd:2 * gd], preferred_element_type=F32).astype(BF16)
    p_mid = jnp.dot(f_ref[half:half + BF16_SUBLANES, :], chan_ref[:, 0:gd],
                    preferred_element_type=F32)[0:1, :]
    a = jnp.dot(cos_ref[...], p_even, preferred_element_type=F32) + sign_ref[...] * p_mid
    b = jnp.dot(sin_ref[...], q_odd, preferred_element_type=F32)
    o_ref[0:half, :] = (a[0:half] - b).astype(o_ref.dtype)

    z = (a[0:half] + b).astype(BF16)
    upper = _flip_tiles(
        exch,
        [z[half - (u + 1) * FLIP_TILE:half - u * FLIP_TILE, :] for u in range(nt)],
        [a[half:half + 1, :] if u == 0 else _first_row(z, half - u * FLIP_TILE) for u in range(nt)])
    for u in range(nt):
        o_ref[half + u * FLIP_TILE:half + (u + 1) * FLIP_TILE, :] = upper[u].astype(o_ref.dtype)


def _fourier_mix(f, batch, seq, consts, riders):
    t, width = f.shape
    gd = FOURIER_GROUP_DIM
    groups = width // gd
    r_in, r_shapes, r_out = _rider_specs(riders, batch * groups, groups)
    outs = pl.pallas_call(
        functools.partial(_fourier_kernel, n_riders=len(riders)),
        out_shape=[jax.ShapeDtypeStruct((t, width), BF16)] + r_shapes,
        grid=(batch, groups),
        in_specs=[pl.BlockSpec((seq, gd), lambda b, g: (b, g))]
                 + [pl.BlockSpec(c.shape, lambda b, g: (0, 0)) for c in consts] + r_in,
        out_specs=[pl.BlockSpec((seq, gd), lambda b, g: (b, g))] + r_out,
        compiler_params=_params(("arbitrary", "arbitrary")),
        name="fourier_mix",
    )(f, *consts, *riders)
    return outs[0], outs[1:]


def _attn_kernel(*refs, lam_init, tq, n_riders):
    lam_ref, g_ref, q_ref, k_ref, v_ref = refs[:5]
    o_ref = refs[5 + n_riders]
    s0, s1, p0, p1, l0, l1 = refs[6 + 2 * n_riders:]
    _cast_riders(refs[5:5 + n_riders], refs[6 + n_riders:6 + 2 * n_riders])
    nq = q_ref.shape[0] // tq
    s_buf, p_buf, l_buf = (s0, s1), (p0, p1), (l0, l1)
    lp = lam_ref[...]
    lam = (jnp.exp(jnp.sum(lp[0:1] * lp[1:2], axis=-1, keepdims=True))
           - jnp.exp(jnp.sum(lp[2:3] * lp[3:4], axis=-1, keepdims=True)) + lam_init)

    def scores(j, slot):
        q = q_ref[j * tq:(j + 1) * tq, :]
        for c in range(2):
            sl = slice(c * HEAD_DIM, (c + 1) * HEAD_DIM)
            s_buf[slot][c] = lax.dot_general(q[:, sl], k_ref[:, sl], (((1,), (1,)), ((), ())),
                                             preferred_element_type=F32)

    def softmax(slot):
        for c in range(2):
            for r in range(0, tq, BF16_SUBLANES):
                rows = slice(r, r + BF16_SUBLANES)
                s = s_buf[slot][c, rows, :]
                p = jnp.exp2(s - jnp.max(s, axis=-1, keepdims=True))
                l_buf[slot][c, rows, :] = jnp.broadcast_to(jnp.sum(p, axis=-1, keepdims=True),
                                                           (BF16_SUBLANES, LANES))
                p_buf[slot][c, rows, :] = p.astype(BF16)

    def values(j, slot):
        o = [jnp.dot(p_buf[slot][c], v_ref[...], preferred_element_type=F32)
             / jnp.concatenate([l_buf[slot][c]] * (v_ref.shape[1] // LANES), axis=1)
             for c in range(2)]
        o = o[0] - lam * o[1]
        ms = jnp.mean(o * o, axis=-1, keepdims=True)
        o = o * lax.rsqrt(ms + RMS_EPS) * g_ref[...] * (1.0 - lam_init)
        o_ref[j * tq:(j + 1) * tq, :] = o.astype(o_ref.dtype)

    scores(0, 0)
    for j in range(nq):
        if j + 1 < nq:
            scores(j + 1, (j + 1) % 2)
        softmax(j % 2)
        if j >= 1:
            values(j - 1, (j - 1) % 2)
    values(nq - 1, (nq - 1) % 2)


def _diff_attention(qk, v, lam_params, sub_g, batch, seq, lam_init, riders, tq=256):
    t = qk.shape[0]
    hw = 2 * HEAD_DIM
    r_in, r_shapes, r_out = _rider_specs(riders, batch * HEADS, HEADS)
    outs = pl.pallas_call(
        functools.partial(_attn_kernel, lam_init=lam_init, tq=tq, n_riders=len(riders)),
        out_shape=[jax.ShapeDtypeStruct((t, ATTN_WIDTH), BF16)] + r_shapes,
        grid=(batch, HEADS),
        in_specs=[pl.BlockSpec((4, HEAD_DIM), lambda b, h: (0, 0)),
                  pl.BlockSpec((1, hw), lambda b, h: (0, 0)),
                  pl.BlockSpec((seq, hw), lambda b, h: (b, h)),
                  pl.BlockSpec((seq, hw), lambda b, h: (b, HEADS + h)),
                  pl.BlockSpec((seq, hw), lambda b, h: (b, h))] + r_in,
        out_specs=[pl.BlockSpec((seq, hw), lambda b, h: (b, h))] + r_out,
        scratch_shapes=[pltpu.VMEM((2, tq, seq), F32), pltpu.VMEM((2, tq, seq), F32),
                        pltpu.VMEM((2, tq, seq), BF16), pltpu.VMEM((2, tq, seq), BF16),
                        pltpu.VMEM((2, tq, LANES), F32), pltpu.VMEM((2, tq, LANES), F32)],
        compiler_params=_params(("arbitrary", "arbitrary")),
        name="diff_attention",
    )(lam_params, sub_g, qk, qk, v, *riders)
    return outs[0], outs[1:]


def _merge_kernel(yf_ref, o_ref_in, wf_ref, wa_ref, gf_ref, ga_ref, out_ref):
    y_f = jnp.dot(yf_ref[...], wf_ref[...], preferred_element_type=F32)
    y_a = jnp.dot(o_ref_in[...], wa_ref[...], preferred_element_type=F32)
    out_ref[...] = (gf_ref[...].astype(F32) * y_f + ga_ref[...].astype(F32) * y_a).astype(out_ref.dtype)


def _merge(yf, oa, wf, wa, gates, bm=1024, bn=512):
    t, kf = yf.shape
    ka = oa.shape[1]
    d = wf.shape[1]
    ga_off = d // bn
    return pl.pallas_call(
        _merge_kernel,
        out_shape=jax.ShapeDtypeStruct((t, d), BF16),
        grid=(t // bm, d // bn),
        in_specs=[pl.BlockSpec((bm, kf), lambda i, j: (i, 0)),
                  pl.BlockSpec((bm, ka), lambda i, j: (i, 0)),
                  pl.BlockSpec((kf, bn), lambda i, j: (0, j)),
                  pl.BlockSpec((ka, bn), lambda i, j: (0, j)),
                  pl.BlockSpec((bm, bn), lambda i, j: (i, j)),
                  pl.BlockSpec((bm, bn), lambda i, j: (i, j + ga_off))],
        out_specs=pl.BlockSpec((bm, bn), lambda i, j: (i, j)),
        compiler_params=_params(("arbitrary", "arbitrary")),
        name="merge_proj",
    )(yf, oa, wf, wa, gates, gates)


def _mm_residual_kernel(a_ref, w_ref, r_ref, o_ref):
    acc = jnp.dot(a_ref[...], w_ref[...], preferred_element_type=F32)
    o_ref[...] = r_ref[...] + acc


def _mm_residual(a, w, res, bm=512, bn=512):
    t, k = a.shape
    n = w.shape[1]
    return pl.pallas_call(
        _mm_residual_kernel,
        out_shape=jax.ShapeDtypeStruct((t, n), F32),
        grid=(t // bm, n // bn),
        in_specs=[pl.BlockSpec((bm, k), lambda i, j: (i, 0)),
                  pl.BlockSpec((k, bn), lambda i, j: (0, j)),
                  pl.BlockSpec((bm, bn), lambda i, j: (i, j))],
        out_specs=pl.BlockSpec((bm, bn), lambda i, j: (i, j)),
        compiler_params=_params(("arbitrary", "arbitrary")),
        name="mm_residual",
    )(a, w, res)


def _mm_residual_stats_kernel(a_ref, w_ref, r_ref, g_ref, h_ref, hg_ref, ssq_ref):
    h = r_ref[...] + jnp.dot(a_ref[...], w_ref[...], preferred_element_type=F32)
    h_ref[...] = h
    hg_ref[...] = (h * g_ref[...]).astype(hg_ref.dtype)

    @pl.when(pl.program_id(1) == 0)
    def _():
        ssq_ref[...] = jnp.zeros_like(ssq_ref)

    ssq_ref[...] += jnp.sum(h * h, axis=-1, keepdims=True)


def _mm_residual_stats(a, w, res, g_row, bm=1024, bn=512):
    t, k = a.shape
    n = w.shape[1]
    return pl.pallas_call(
        _mm_residual_stats_kernel,
        out_shape=(jax.ShapeDtypeStruct((t, n), F32),
                   jax.ShapeDtypeStruct((t, n), BF16),
                   jax.ShapeDtypeStruct((t, 1), F32)),
        grid=(t // bm, n // bn),
        in_specs=[pl.BlockSpec((bm, k), lambda i, j: (i, 0)),
                  pl.BlockSpec((k, bn), lambda i, j: (0, j)),
                  pl.BlockSpec((bm, bn), lambda i, j: (i, j)),
                  pl.BlockSpec((1, bn), lambda i, j: (0, j))],
        out_specs=(pl.BlockSpec((bm, bn), lambda i, j: (i, j)),
                   pl.BlockSpec((bm, bn), lambda i, j: (i, j)),
                   pl.BlockSpec((bm, 1), lambda i, j: (i, 0))),
        compiler_params=_params(("arbitrary", "arbitrary")),
        name="mm_residual_stats",
    )(a, w, res, g_row)


def _swiglu_kernel(hg_ref, ssq_ref, wg_ref, wu_ref, o_ref):
    hg = hg_ref[...]
    rstd = lax.rsqrt(ssq_ref[...] * (1.0 / hg.shape[1]) + RMS_EPS)
    gate = jnp.dot(hg, wg_ref[...], preferred_element_type=F32) * rstd
    up = jnp.dot(hg, wu_ref[...], preferred_element_type=F32) * rstd
    o_ref[...] = (gate * _sigmoid(gate) * up).astype(o_ref.dtype)


def _swiglu(hg, ssq, wg, wu, bm=1024, bn=256):
    t, k = hg.shape
    n = wg.shape[1]
    return pl.pallas_call(
        _swiglu_kernel,
        out_shape=jax.ShapeDtypeStruct((t, n), BF16),
        grid=(t // bm, n // bn),
        in_specs=[pl.BlockSpec((bm, k), lambda i, j: (i, 0)),
                  pl.BlockSpec((bm, 1), lambda i, j: (i, 0)),
                  pl.BlockSpec((k, bn), lambda i, j: (0, j)),
                  pl.BlockSpec((k, bn), lambda i, j: (0, j))],
        out_specs=pl.BlockSpec((bm, bn), lambda i, j: (i, j)),
        compiler_params=_params(("arbitrary", "arbitrary")),
        name="swiglu_up",
    )(hg, ssq, wg, wu)


def kernel(x, positions, norm_mix_g, w_in, b_gate, lambda_q1, lambda_k1, lambda_q2, lambda_k2,
           subln_g, w_fourier_out, w_attn_out, w_out, norm_ffn_g, w_ffn_gate, w_ffn_up,
           w_ffn_down, norm_final_g):
    batch, seq, d = x.shape
    depth = w_in.shape[0]
    t = batch * seq
    h = x.reshape(t, d)

    inv_freq = ROPE_THETA ** (-jnp.arange(0, HEAD_DIM, 2, dtype=F32) / HEAD_DIM)
    invf_full = jnp.concatenate([inv_freq, inv_freq]).reshape(1, HEAD_DIM)
    pos_col = positions.reshape(t, 1)
    qk_scale = jnp.concatenate([jnp.full((1, ATTN_WIDTH), math.log2(math.e) / math.sqrt(HEAD_DIM), F32),
                                jnp.ones((1, ATTN_WIDTH), F32)], axis=1)
    dft_consts = _dft_constants(seq)

    for l in range(depth):
        lam_init = 0.8 - 0.6 * math.exp(-0.3 * l)
        w_in_b = w_in[l].astype(BF16)
        if l == 0:
            u, cos_t, sin_t = _rmsnorm(h, norm_mix_g[l].reshape(1, d), BF16, rope=(pos_col, invf_full))
        else:
            u = _rmsnorm(h, norm_mix_g[l].reshape(1, d), BF16)
        f = _in_proj(u, w_in_b, 0, FOURIER_WIDTH, "cast")
        qk = _in_proj(u, w_in_b, FOURIER_WIDTH, 2 * ATTN_WIDTH, "rope", (cos_t, sin_t, qk_scale))
        v = _in_proj(u, w_in_b, FOURIER_WIDTH + 2 * ATTN_WIDTH, ATTN_WIDTH, "cast")
        gates = _in_proj(u, w_in_b, FOURIER_WIDTH + 3 * ATTN_WIDTH, 2 * d, "sigmoid",
                         (b_gate[l].reshape(1, 2 * d),))

        y_f, (w_f_b, w_a_b, w_down_b) = _fourier_mix(
            f, batch, seq, dft_consts, [w_fourier_out[l], w_attn_out[l], w_ffn_down[l]])
        lam_params = jnp.stack([lambda_q1[l], lambda_k1[l], lambda_q2[l], lambda_k2[l]]).astype(F32)
        o_a, (w_gate_b, w_up_b, w_out_b) = _diff_attention(
            qk, v, lam_params, subln_g[l].reshape(1, 2 * HEAD_DIM).astype(F32), batch, seq, lam_init,
            [w_ffn_gate[l], w_ffn_up[l], w_out[l]])

        merged = _merge(y_f, o_a, w_f_b, w_a_b, gates)
        h, hg, ssq = _mm_residual_stats(merged, w_out_b, h, norm_ffn_g[l].reshape(1, d).astype(F32))
        hid = _swiglu(hg, ssq, w_gate_b, w_up_b)
        h = _mm_residual(hid, w_down_b, h)

    out = _rmsnorm(h, norm_final_g.reshape(1, d), x.dtype)
    return out.reshape(batch, seq, d)
```

```python
import functools
import math

import numpy as np
import jax
import jax.numpy as jnp
from jax import lax
from jax.experimental import pallas as pl
from jax.experimental.pallas import tpu as pltpu

F32 = jnp.float32
BF16 = jnp.bfloat16

D_MODEL = 4096
FOURIER_WIDTH = D_MODEL // 2
FOURIER_GROUPS = 8
FOURIER_GROUP_DIM = FOURIER_WIDTH // FOURIER_GROUPS
HEAD_DIM = 128
HEADS = (D_MODEL // 2) // (2 * HEAD_DIM)
ATTN_WIDTH = HEADS * 2 * HEAD_DIM
FFN_HIDDEN = -(-8 * D_MODEL // (3 * 256)) * 256
ROPE_THETA = 10000.0
RMS_EPS = 1e-6

LANES = 128
BF16_SUBLANES = 16
VMEM_LIMIT = 56 * 1024 * 1024


def _params(sem):
    return pltpu.CompilerParams(dimension_semantics=sem, vmem_limit_bytes=VMEM_LIMIT)


def _sigmoid(x):
    return 0.5 * jnp.tanh(0.5 * x) + 0.5


def _rider_rows(rows, steps):
    for nblk in range(steps, 0, -1):
        if rows % nblk == 0 and (rows // nblk) % BF16_SUBLANES == 0:
            return rows // nblk
    raise ValueError(f"cannot split {rows} rows into at most {steps} bf16-tileable blocks")


def _rider_specs(riders, steps, inner):
    arrays, in_specs, out_shapes, out_specs = [], [], [], []
    for rider in riders:
        w, col0, ncols = rider if isinstance(rider, tuple) else (rider, 0, rider.shape[1])
        assert col0 % ncols == 0
        rb = _rider_rows(w.shape[0], steps)
        last = w.shape[0] // rb - 1
        in_index = lambda a, b, last=last, cb=col0 // ncols: (jnp.minimum(a * inner + b, last), cb)
        out_index = lambda a, b, last=last: (jnp.minimum(a * inner + b, last), 0)
        arrays.append(w)
        in_specs.append(pl.BlockSpec((rb, ncols), in_index))
        out_specs.append(pl.BlockSpec((rb, ncols), out_index))
        out_shapes.append(jax.ShapeDtypeStruct((w.shape[0], ncols), BF16))
    return arrays, in_specs, out_shapes, out_specs


def _cast_riders(rider_in, rider_out):
    for w_ref, o_ref in zip(rider_in, rider_out):
        o_ref[...] = w_ref[...].astype(o_ref.dtype)


def _rope_tables(pos_ref, invf_ref, cos_ref, sin_ref):
    ang = pos_ref[...].astype(F32) * invf_ref[...]
    lane = lax.broadcasted_iota(jnp.int32, ang.shape, 1)
    s = jnp.sin(ang)
    cos_ref[...] = jnp.cos(ang)
    sin_ref[...] = jnp.where(lane < HEAD_DIM // 2, -s, s)


def _rmsnorm_kernel(x_ref, g_ref, o_ref):
    x = x_ref[...].astype(F32)
    ms = jnp.mean(x * x, axis=-1, keepdims=True)
    o_ref[...] = (x * lax.rsqrt(ms + RMS_EPS) * g_ref[...]).astype(o_ref.dtype)


def _rmsnorm_rope_kernel(x_ref, g_ref, pos_ref, invf_ref, o_ref, cos_ref, sin_ref):
    _rmsnorm_kernel(x_ref, g_ref, o_ref)
    _rope_tables(pos_ref, invf_ref, cos_ref, sin_ref)


def _rmsnorm(x, g_row, out_dtype, rope=None, tm=256):
    t, d = x.shape
    row = lambda i: (i, 0)
    fixed = lambda i: (0, 0)
    in_specs = [pl.BlockSpec((tm, d), row), pl.BlockSpec((1, d), fixed)]
    out_shape = jax.ShapeDtypeStruct((t, d), out_dtype)
    out_specs = pl.BlockSpec((tm, d), row)
    if rope is None:
        body, args = _rmsnorm_kernel, (x, g_row)
    else:
        body, args = _rmsnorm_rope_kernel, (x, g_row, *rope)
        in_specs += [pl.BlockSpec((tm, 1), row), pl.BlockSpec((1, HEAD_DIM), fixed)]
        out_shape = (out_shape,) + (jax.ShapeDtypeStruct((t, HEAD_DIM), F32),) * 2
        out_specs = (out_specs,) + (pl.BlockSpec((tm, HEAD_DIM), row),) * 2
    return pl.pallas_call(
        body,
        out_shape=out_shape,
        grid=(t // tm,),
        in_specs=in_specs,
        out_specs=out_specs,
        compiler_params=_params(("arbitrary",)),
        name="rmsnorm",
    )(*args)


IN_PROJ_EXTRAS = {"cast": 0, "rope": 3, "sigmoid": 1}


def _in_proj_kernel(*refs, kind, n_riders):
    n_in = 2 + IN_PROJ_EXTRAS[kind]
    a_ref, w_ref = refs[:2]
    o_ref = refs[n_in + n_riders]
    _cast_riders(refs[n_in:n_in + n_riders], refs[n_in + n_riders + 1:])
    acc = jnp.dot(a_ref[...], w_ref[...], preferred_element_type=F32)
    if kind == "cast":
        o_ref[...] = acc.astype(o_ref.dtype)
    elif kind == "rope":
        cos_ref, sin_ref, cs_ref = refs[2:n_in]
        cos = cos_ref[...]
        sin = sin_ref[...]
        for c in range(acc.shape[1] // HEAD_DIM):
            sl = slice(c * HEAD_DIM, (c + 1) * HEAD_DIM)
            xc = acc[:, sl]
            r = xc * cos + pltpu.roll(xc, HEAD_DIM // 2, 1) * sin
            o_ref[:, sl] = (r * cs_ref[:, sl]).astype(o_ref.dtype)
    else:
        o_ref[...] = _sigmoid(acc + refs[2][...]).astype(o_ref.dtype)


def _in_proj(u, w, col0, ncols, kind, extras=(), riders=(), bm=1024, bn=1024):
    t, k = u.shape
    joff = col0 // bn
    grid = (t // bm, ncols // bn)
    in_specs = [pl.BlockSpec((bm, k), lambda i, j: (i, 0)),
                pl.BlockSpec((k, bn), lambda i, j: (0, j + joff))]
    if kind == "rope":
        in_specs += [pl.BlockSpec((bm, HEAD_DIM), lambda i, j: (i, 0)),
                     pl.BlockSpec((bm, HEAD_DIM), lambda i, j: (i, 0)),
                     pl.BlockSpec((1, bn), lambda i, j: (0, j))]
    elif kind == "sigmoid":
        in_specs += [pl.BlockSpec((1, bn), lambda i, j: (0, j))]
    assert len(extras) == IN_PROJ_EXTRAS[kind]
    r_arrays, r_in, r_shapes, r_out = _rider_specs(riders, grid[0] * grid[1], grid[1])
    outs = pl.pallas_call(
        functools.partial(_in_proj_kernel, kind=kind, n_riders=len(riders)),
        out_shape=[jax.ShapeDtypeStruct((t, ncols), BF16)] + r_shapes,
        grid=grid,
        in_specs=in_specs + r_in,
        out_specs=[pl.BlockSpec((bm, bn), lambda i, j: (i, j))] + r_out,
        compiler_params=_params(("arbitrary", "arbitrary")),
        name="in_proj_" + kind,
    )(u, w, *extras, *r_arrays)
    return outs[0], outs[1:]


FLIP_TILE = 256


def _dft_constants(seq):
    gd = FOURIER_GROUP_DIM
    half = seq // 2
    jc = np.arange(gd, dtype=np.int64)
    ang_c = 2.0 * np.pi * ((jc[:, None] * jc[None, :]) % gd) / gd
    chan = np.concatenate([np.cos(ang_c), np.sin(ang_c)], axis=1) / math.sqrt(gd)
    j = np.arange(half + BF16_SUBLANES, dtype=np.int64)[:, None]
    n = np.arange(half, dtype=np.int64)[None, :]
    ang_s = 2.0 * np.pi * ((j * n) % seq) / seq
    cos_m = np.cos(ang_s) / math.sqrt(seq)
    cos_m[:, 0] *= 0.5
    cos_m[half + 1:, :] = 0.0
    sin_m = np.sin(ang_s[:half]) / math.sqrt(seq)
    sign = np.where(np.arange(half + BF16_SUBLANES) % 2 == 0, 1.0, -1.0) / math.sqrt(seq)
    sign[half + 1:] = 0.0
    r = np.arange(1, FLIP_TILE)
    exch = np.zeros((FLIP_TILE, FLIP_TILE))
    exch[r, FLIP_TILE - r] = 1.0
    return (jnp.asarray(chan, BF16), jnp.asarray(cos_m, BF16), jnp.asarray(sin_m, BF16),
            jnp.asarray(sign.reshape(-1, 1), F32), jnp.asarray(exch, BF16))


def _flip_tiles(exch, tiles, first_rows):
    row = lax.broadcasted_iota(jnp.int32, tiles[0].shape, 0)
    return [jnp.where(row == 0, first, jnp.dot(exch, tile, preferred_element_type=F32))
            for tile, first in zip(tiles, first_rows)]


def _first_row(x, r):
    return x[r:r + BF16_SUBLANES, :].astype(F32)[0:1, :]


def _fourier_kernel(*refs, n_riders):
    f_ref, chan_ref, cos_ref, sin_ref, sign_ref, exch_ref = refs[:6]
    o_ref = refs[6 + n_riders]
    _cast_riders(refs[6:6 + n_riders], refs[7 + n_riders:7 + 2 * n_riders])
    seq, gd = f_ref.shape
    half = seq // 2
    nt = half // FLIP_TILE
    exch = exch_ref[...]

    rev = jnp.concatenate(_flip_tiles(
        exch,
        [f_ref[seq - (t + 1) * FLIP_TILE:seq - t * FLIP_TILE, :] for t in range(nt)],
        [_first_row(f_ref, 0 if t == 0 else seq - t * FLIP_TILE) for t in range(nt)]), axis=0)
    x_lo = f_ref[0:half, :].astype(F32)
    even = (x_lo + rev).astype(BF16)
    odd = (x_lo - rev).astype(BF16)
    p_even = jnp.dot(even, chan_ref[:, 0:gd], preferred_element_type=F32).astype(BF16)
    q_odd = jnp.dot(odd, chan_ref[:, gd:2 * gd], preferred_element_type=F32).astype(BF16)
    p_mid = jnp.dot(f_ref[half:half + BF16_SUBLANES, :], chan_ref[:, 0:gd],
                    preferred_element_type=F32)[0:1, :]
    a = jnp.dot(cos_ref[...], p_even, preferred_element_type=F32) + sign_ref[...] * p_mid
    b = jnp.dot(sin_ref[...], q_odd, preferred_element_type=F32)
    o_ref[0:half, :] = (a[0:half] - b).astype(o_ref.dtype)

    z = (a[0:half] + b).astype(BF16)
    upper = _flip_tiles(
        exch,
        [z[half - (u + 1) * FLIP_TILE:half - u * FLIP_TILE, :] for u in range(nt)],
        [a[half:half + 1, :] if u == 0 else _first_row(z, half - u * FLIP_TILE) for u in range(nt)])
    for u in range(nt):
        o_ref[half + u * FLIP_TILE:half + (u + 1) * FLIP_TILE, :] = upper[u].astype(o_ref.dtype)


def _fourier_mix(f, batch, seq, consts, riders):
    t, width = f.shape
    gd = FOURIER_GROUP_DIM
    groups = width // gd
    r_arrays, r_in, r_shapes, r_out = _rider_specs(riders, batch * groups, groups)
    outs = pl.pallas_call(
        functools.partial(_fourier_kernel, n_riders=len(riders)),
        out_shape=[jax.ShapeDtypeStruct((t, width), BF16)] + r_shapes,
        grid=(batch, groups),
        in_specs=[pl.BlockSpec((seq, gd), lambda b, g: (b, g))]
                 + [pl.BlockSpec(c.shape, lambda b, g: (0, 0)) for c in consts] + r_in,
        out_specs=[pl.BlockSpec((seq, gd), lambda b, g: (b, g))] + r_out,
        compiler_params=_params(("arbitrary", "arbitrary")),
        name="fourier_mix",
    )(f, *consts, *r_arrays)
    return outs[0], outs[1:]


def _attn_kernel(*refs, lam_init, tq, n_riders):
    lam_ref, g_ref, q_ref, k_ref, v_ref = refs[:5]
    o_ref = refs[5 + n_riders]
    s0, s1, p0, p1, l0, l1 = refs[6 + 2 * n_riders:]
    _cast_riders(refs[5:5 + n_riders], refs[6 + n_riders:6 + 2 * n_riders])
    nq = q_ref.shape[0] // tq
    s_buf, p_buf, l_buf = (s0, s1), (p0, p1), (l0, l1)
    lp = lam_ref[...]
    lam = (jnp.exp(jnp.sum(lp[0:1] * lp[1:2], axis=-1, keepdims=True))
           - jnp.exp(jnp.sum(lp[2:3] * lp[3:4], axis=-1, keepdims=True)) + lam_init)

    def scores(j, slot):
        q = q_ref[j * tq:(j + 1) * tq, :]
        for c in range(2):
            sl = slice(c * HEAD_DIM, (c + 1) * HEAD_DIM)
            s_buf[slot][c] = lax.dot_general(q[:, sl], k_ref[:, sl], (((1,), (1,)), ((), ())),
                                             preferred_element_type=F32)

    def softmax(slot):
        for c in range(2):
            for r in range(0, tq, BF16_SUBLANES):
                rows = slice(r, r + BF16_SUBLANES)
                s = s_buf[slot][c, rows, :]
                p = jnp.exp2(s - jnp.max(s, axis=-1, keepdims=True))
                l_buf[slot][c, rows, :] = jnp.broadcast_to(jnp.sum(p, axis=-1, keepdims=True),
                                                           (BF16_SUBLANES, LANES))
                p_buf[slot][c, rows, :] = p.astype(BF16)

    def values(j, slot):
        o = [jnp.dot(p_buf[slot][c], v_ref[...], preferred_element_type=F32)
             / jnp.concatenate([l_buf[slot][c]] * (v_ref.shape[1] // LANES), axis=1)
             for c in range(2)]
        o = o[0] - lam * o[1]
        ms = jnp.mean(o * o, axis=-1, keepdims=True)
        o = o * lax.rsqrt(ms + RMS_EPS) * g_ref[...] * (1.0 - lam_init)
        o_ref[j * tq:(j + 1) * tq, :] = o.astype(o_ref.dtype)

    scores(0, 0)
    for j in range(nq):
        if j + 1 < nq:
            scores(j + 1, (j + 1) % 2)
        softmax(j % 2)
        if j >= 1:
            values(j - 1, (j - 1) % 2)
    values(nq - 1, (nq - 1) % 2)


def _diff_attention(qk, v, lam_params, sub_g, batch, seq, lam_init, riders, tq=256):
    t = qk.shape[0]
    hw = 2 * HEAD_DIM
    r_arrays, r_in, r_shapes, r_out = _rider_specs(riders, batch * HEADS, HEADS)
    outs = pl.pallas_call(
        functools.partial(_attn_kernel, lam_init=lam_init, tq=tq, n_riders=len(riders)),
        out_shape=[jax.ShapeDtypeStruct((t, ATTN_WIDTH), BF16)] + r_shapes,
        grid=(batch, HEADS),
        in_specs=[pl.BlockSpec((4, HEAD_DIM), lambda b, h: (0, 0)),
                  pl.BlockSpec((1, hw), lambda b, h: (0, 0)),
                  pl.BlockSpec((seq, hw), lambda b, h: (b, h)),
                  pl.BlockSpec((seq, hw), lambda b, h: (b, HEADS + h)),
                  pl.BlockSpec((seq, hw), lambda b, h: (b, h))] + r_in,
        out_specs=[pl.BlockSpec((seq, hw), lambda b, h: (b, h))] + r_out,
        scratch_shapes=[pltpu.VMEM((2, tq, seq), F32), pltpu.VMEM((2, tq, seq), F32),
                        pltpu.VMEM((2, tq, seq), BF16), pltpu.VMEM((2, tq, seq), BF16),
                        pltpu.VMEM((2, tq, LANES), F32), pltpu.VMEM((2, tq, LANES), F32)],
        compiler_params=_params(("arbitrary", "arbitrary")),
        name="diff_attention",
    )(lam_params, sub_g, qk, qk, v, *r_arrays)
    return outs[0], outs[1:]


def _merge_kernel(yf_ref, o_ref_in, wf_ref, wa_ref, gf_ref, ga_ref, out_ref):
    y_f = jnp.dot(yf_ref[...], wf_ref[...], preferred_element_type=F32)
    y_a = jnp.dot(o_ref_in[...], wa_ref[...], preferred_element_type=F32)
    out_ref[...] = (gf_ref[...].astype(F32) * y_f + ga_ref[...].astype(F32) * y_a).astype(out_ref.dtype)


def _merge(yf, oa, wf, wa, gates, bm=1024, bn=512):
    t, kf = yf.shape
    ka = oa.shape[1]
    d = wf.shape[1]
    ga_off = d // bn
    return pl.pallas_call(
        _merge_kernel,
        out_shape=jax.ShapeDtypeStruct((t, d), BF16),
        grid=(t // bm, d // bn),
        in_specs=[pl.BlockSpec((bm, kf), lambda i, j: (i, 0)),
                  pl.BlockSpec((bm, ka), lambda i, j: (i, 0)),
                  pl.BlockSpec((kf, bn), lambda i, j: (0, j)),
                  pl.BlockSpec((ka, bn), lambda i, j: (0, j)),
                  pl.BlockSpec((bm, bn), lambda i, j: (i, j)),
                  pl.BlockSpec((bm, bn), lambda i, j: (i, j + ga_off))],
        out_specs=pl.BlockSpec((bm, bn), lambda i, j: (i, j)),
        compiler_params=_params(("arbitrary", "arbitrary")),
        name="merge_proj",
    )(yf, oa, wf, wa, gates, gates)


def _mm_residual_kernel(a_ref, w_ref, r_ref, o_ref):
    acc = jnp.dot(a_ref[...], w_ref[...], preferred_element_type=F32)
    o_ref[...] = r_ref[...] + acc


def _mm_residual(a, w, res, bm=512, bn=512):
    t, k = a.shape
    n = w.shape[1]
    return pl.pallas_call(
        _mm_residual_kernel,
        out_shape=jax.ShapeDtypeStruct((t, n), F32),
        grid=(t // bm, n // bn),
        in_specs=[pl.BlockSpec((bm, k), lambda i, j: (i, 0)),
                  pl.BlockSpec((k, bn), lambda i, j: (0, j)),
                  pl.BlockSpec((bm, bn), lambda i, j: (i, j))],
        out_specs=pl.BlockSpec((bm, bn), lambda i, j: (i, j)),
        compiler_params=_params(("arbitrary", "arbitrary")),
        name="mm_residual",
    )(a, w, res)


def _mm_residual_stats_kernel(a_ref, w_ref, r_ref, g_ref, h_ref, hg_ref, ssq_ref):
    h = r_ref[...] + jnp.dot(a_ref[...], w_ref[...], preferred_element_type=F32)
    h_ref[...] = h
    hg_ref[...] = (h * g_ref[...]).astype(hg_ref.dtype)

    @pl.when(pl.program_id(1) == 0)
    def _():
        ssq_ref[...] = jnp.zeros_like(ssq_ref)

    ssq_ref[...] += jnp.sum(h * h, axis=-1, keepdims=True)


def _mm_residual_stats(a, w, res, g_row, bm=1024, bn=512):
    t, k = a.shape
    n = w.shape[1]
    return pl.pallas_call(
        _mm_residual_stats_kernel,
        out_shape=(jax.ShapeDtypeStruct((t, n), F32),
                   jax.ShapeDtypeStruct((t, n), BF16),
                   jax.ShapeDtypeStruct((t, 1), F32)),
        grid=(t // bm, n // bn),
        in_specs=[pl.BlockSpec((bm, k), lambda i, j: (i, 0)),
                  pl.BlockSpec((k, bn), lambda i, j: (0, j)),
                  pl.BlockSpec((bm, bn), lambda i, j: (i, j)),
                  pl.BlockSpec((1, bn), lambda i, j: (0, j))],
        out_specs=(pl.BlockSpec((bm, bn), lambda i, j: (i, j)),
                   pl.BlockSpec((bm, bn), lambda i, j: (i, j)),
                   pl.BlockSpec((bm, 1), lambda i, j: (i, 0))),
        compiler_params=_params(("arbitrary", "arbitrary")),
        name="mm_residual_stats",
    )(a, w, res, g_row)


def _swiglu_kernel(hg_ref, ssq_ref, wg_ref, wu_ref, o_ref):
    hg = hg_ref[...]
    rstd = lax.rsqrt(ssq_ref[...] * (1.0 / hg.shape[1]) + RMS_EPS)
    gate = jnp.dot(hg, wg_ref[...], preferred_element_type=F32) * rstd
    up = jnp.dot(hg, wu_ref[...], preferred_element_type=F32) * rstd
    o_ref[...] = (gate * _sigmoid(gate) * up).astype(o_ref.dtype)


def _swiglu(hg, ssq, wg, wu, bm=1024, bn=256):
    t, k = hg.shape
    n = wg.shape[1]
    return pl.pallas_call(
        _swiglu_kernel,
        out_shape=jax.ShapeDtypeStruct((t, n), BF16),
        grid=(t // bm, n // bn),
        in_specs=[pl.BlockSpec((bm, k), lambda i, j: (i, 0)),
                  pl.BlockSpec((bm, 1), lambda i, j: (i, 0)),
                  pl.BlockSpec((k, bn), lambda i, j: (0, j)),
                  pl.BlockSpec((k, bn), lambda i, j: (0, j))],
        out_specs=pl.BlockSpec((bm, bn), lambda i, j: (i, j)),
        compiler_params=_params(("arbitrary", "arbitrary")),
        name="swiglu_up",
    )(hg, ssq, wg, wu)


def kernel(x, positions, norm_mix_g, w_in, b_gate, lambda_q1, lambda_k1, lambda_q2, lambda_k2,
           subln_g, w_fourier_out, w_attn_out, w_out, norm_ffn_g, w_ffn_gate, w_ffn_up,
           w_ffn_down, norm_final_g):
    batch, seq, d = x.shape
    depth = w_in.shape[0]
    t = batch * seq
    h = x.reshape(t, d)

    inv_freq = ROPE_THETA ** (-jnp.arange(0, HEAD_DIM, 2, dtype=F32) / HEAD_DIM)
    invf_full = jnp.concatenate([inv_freq, inv_freq]).reshape(1, HEAD_DIM)
    pos_col = positions.reshape(t, 1)
    qk_scale = jnp.concatenate([jnp.full((1, ATTN_WIDTH), math.log2(math.e) / math.sqrt(HEAD_DIM), F32),
                                jnp.ones((1, ATTN_WIDTH), F32)], axis=1)
    dft_consts = _dft_constants(seq)

    for l in range(depth):
        lam_init = 0.8 - 0.6 * math.exp(-0.3 * l)
        mix_cols = FOURIER_WIDTH + 3 * ATTN_WIDTH
        w_mix_b = w_in[l][:, :mix_cols].astype(BF16)
        if l == 0:
            u, cos_t, sin_t = _rmsnorm(h, norm_mix_g[l].reshape(1, d), BF16, rope=(pos_col, invf_full))
        else:
            u = _rmsnorm(h, norm_mix_g[l].reshape(1, d), BF16)
        f, _ = _in_proj(u, w_mix_b, 0, FOURIER_WIDTH, "cast")
        qk, (w_gates_b,) = _in_proj(u, w_mix_b, FOURIER_WIDTH, 2 * ATTN_WIDTH, "rope",
                                    (cos_t, sin_t, qk_scale), riders=[(w_in[l], mix_cols, 2 * d)])
        v, _ = _in_proj(u, w_mix_b, FOURIER_WIDTH + 2 * ATTN_WIDTH, ATTN_WIDTH, "cast")
        gates, (w_down_b,) = _in_proj(u, w_gates_b, 0, 2 * d, "sigmoid", (b_gate[l].reshape(1, 2 * d),),
                                      riders=[w_ffn_down[l]])

        y_f, (w_f_b, w_a_b) = _fourier_mix(
            f, batch, seq, dft_consts, [w_fourier_out[l], w_attn_out[l]])
        lam_params = jnp.stack([lambda_q1[l], lambda_k1[l], lambda_q2[l], lambda_k2[l]]).astype(F32)
        o_a, (w_gate_b, w_up_b, w_out_b) = _diff_attention(
            qk, v, lam_params, subln_g[l].reshape(1, 2 * HEAD_DIM).astype(F32), batch, seq, lam_init,
            [w_ffn_gate[l], w_ffn_up[l], w_out[l]])

        merged = _merge(y_f, o_a, w_f_b, w_a_b, gates)
        h, hg, ssq = _mm_residual_stats(merged, w_out_b, h, norm_ffn_g[l].reshape(1, d).astype(F32))
        hid = _swiglu(hg, ssq, w_gate_b, w_up_b)
        h = _mm_residual(hid, w_down_b, h)

    out = _rmsnorm(h, norm_final_g.reshape(1, d), x.dtype)
    return out.reshape(batch, seq, d)
```

```python
import functools
import math

import numpy as np
import jax
import jax.numpy as jnp
from jax import lax
from jax.experimental import pallas as pl
from jax.experimental.pallas import tpu as pltpu

F32 = jnp.float32
BF16 = jnp.bfloat16

D_MODEL = 4096
FOURIER_WIDTH = D_MODEL // 2
FOURIER_GROUPS = 8
FOURIER_GROUP_DIM = FOURIER_WIDTH // FOURIER_GROUPS
HEAD_DIM = 128
HEADS = (D_MODEL // 2) // (2 * HEAD_DIM)
ATTN_WIDTH = HEADS * 2 * HEAD_DIM
FFN_HIDDEN = -(-8 * D_MODEL // (3 * 256)) * 256
ROPE_THETA = 10000.0
RMS_EPS = 1e-6

LANES = 128
BF16_SUBLANES = 16
VMEM_LIMIT = 56 * 1024 * 1024


def _params(sem):
    return pltpu.CompilerParams(dimension_semantics=sem, vmem_limit_bytes=VMEM_LIMIT)


def _sigmoid(x):
    return 0.5 * jnp.tanh(0.5 * x) + 0.5


def _rider_rows(rows, steps):
    for nblk in range(steps, 0, -1):
        if rows % nblk == 0 and (rows // nblk) % BF16_SUBLANES == 0:
            return rows // nblk
    raise ValueError(f"cannot split {rows} rows into at most {steps} bf16-tileable blocks")


def _rider_specs(riders, steps, inner):
    arrays, in_specs, out_shapes, out_specs = [], [], [], []
    for rider in riders:
        w, col0, ncols = rider if isinstance(rider, tuple) else (rider, 0, rider.shape[1])
        assert col0 % ncols == 0
        rb = _rider_rows(w.shape[0], steps)
        last = w.shape[0] // rb - 1
        in_index = lambda a, b, last=last, cb=col0 // ncols: (jnp.minimum(a * inner + b, last), cb)
        out_index = lambda a, b, last=last: (jnp.minimum(a * inner + b, last), 0)
        arrays.append(w)
        in_specs.append(pl.BlockSpec((rb, ncols), in_index))
        out_specs.append(pl.BlockSpec((rb, ncols), out_index))
        out_shapes.append(jax.ShapeDtypeStruct((w.shape[0], ncols), BF16))
    return arrays, in_specs, out_shapes, out_specs


def _cast_riders(rider_in, rider_out):
    for w_ref, o_ref in zip(rider_in, rider_out):
        o_ref[...] = w_ref[...].astype(o_ref.dtype)


def _rope_tables(pos_ref, invf_ref, cos_ref, sin_ref):
    ang = pos_ref[...].astype(F32) * invf_ref[...]
    lane = lax.broadcasted_iota(jnp.int32, ang.shape, 1)
    s = jnp.sin(ang)
    cos_ref[...] = jnp.cos(ang)
    sin_ref[...] = jnp.where(lane < HEAD_DIM // 2, -s, s)


def _rmsnorm_kernel(x_ref, g_ref, o_ref):
    x = x_ref[...].astype(F32)
    ms = jnp.mean(x * x, axis=-1, keepdims=True)
    o_ref[...] = (x * lax.rsqrt(ms + RMS_EPS) * g_ref[...]).astype(o_ref.dtype)


def _rmsnorm_rope_kernel(x_ref, g_ref, pos_ref, invf_ref, o_ref, cos_ref, sin_ref):
    _rmsnorm_kernel(x_ref, g_ref, o_ref)
    _rope_tables(pos_ref, invf_ref, cos_ref, sin_ref)


def _rmsnorm(x, g_row, out_dtype, rope=None, tm=512):
    t, d = x.shape
    row = lambda i: (i, 0)
    fixed = lambda i: (0, 0)
    in_specs = [pl.BlockSpec((tm, d), row), pl.BlockSpec((1, d), fixed)]
    out_shape = jax.ShapeDtypeStruct((t, d), out_dtype)
    out_specs = pl.BlockSpec((tm, d), row)
    if rope is None:
        body, args = _rmsnorm_kernel, (x, g_row)
    else:
        body, args = _rmsnorm_rope_kernel, (x, g_row, *rope)
        in_specs += [pl.BlockSpec((tm, 1), row), pl.BlockSpec((1, HEAD_DIM), fixed)]
        out_shape = (out_shape,) + (jax.ShapeDtypeStruct((t, HEAD_DIM), F32),) * 2
        out_specs = (out_specs,) + (pl.BlockSpec((tm, HEAD_DIM), row),) * 2
    return pl.pallas_call(
        body,
        out_shape=out_shape,
        grid=(t // tm,),
        in_specs=in_specs,
        out_specs=out_specs,
        compiler_params=_params(("arbitrary",)),
        name="rmsnorm",
    )(*args)


IN_PROJ_EXTRAS = {"cast": 0, "rope": 3, "sigmoid": 1}


def _in_proj_kernel(*refs, kind, n_riders):
    n_in = 2 + IN_PROJ_EXTRAS[kind]
    a_ref, w_ref = refs[:2]
    o_ref = refs[n_in + n_riders]
    _cast_riders(refs[n_in:n_in + n_riders], refs[n_in + n_riders + 1:])
    acc = jnp.dot(a_ref[...], w_ref[...], preferred_element_type=F32)
    if kind == "cast":
        o_ref[...] = acc.astype(o_ref.dtype)
    elif kind == "rope":
        cos_ref, sin_ref, cs_ref = refs[2:n_in]
        cos = cos_ref[...]
        sin = sin_ref[...]
        for c in range(acc.shape[1] // HEAD_DIM):
            sl = slice(c * HEAD_DIM, (c + 1) * HEAD_DIM)
            xc = acc[:, sl]
            r = xc * cos + pltpu.roll(xc, HEAD_DIM // 2, 1) * sin
            o_ref[:, sl] = (r * cs_ref[:, sl]).astype(o_ref.dtype)
    else:
        o_ref[...] = _sigmoid(acc + refs[2][...]).astype(o_ref.dtype)


def _in_proj(u, w, col0, ncols, kind, extras=(), riders=(), bm=1024, bn=1024):
    t, k = u.shape
    joff = col0 // bn
    grid = (t // bm, ncols // bn)
    in_specs = [pl.BlockSpec((bm, k), lambda i, j: (i, 0)),
                pl.BlockSpec((k, bn), lambda i, j: (0, j + joff))]
    if kind == "rope":
        in_specs += [pl.BlockSpec((bm, HEAD_DIM), lambda i, j: (i, 0)),
                     pl.BlockSpec((bm, HEAD_DIM), lambda i, j: (i, 0)),
                     pl.BlockSpec((1, bn), lambda i, j: (0, j))]
    elif kind == "sigmoid":
        in_specs += [pl.BlockSpec((1, bn), lambda i, j: (0, j))]
    assert len(extras) == IN_PROJ_EXTRAS[kind]
    r_arrays, r_in, r_shapes, r_out = _rider_specs(riders, grid[0] * grid[1], grid[1])
    outs = pl.pallas_call(
        functools.partial(_in_proj_kernel, kind=kind, n_riders=len(riders)),
        out_shape=[jax.ShapeDtypeStruct((t, ncols), BF16)] + r_shapes,
        grid=grid,
        in_specs=in_specs + r_in,
        out_specs=[pl.BlockSpec((bm, bn), lambda i, j: (i, j))] + r_out,
        compiler_params=_params(("arbitrary", "arbitrary")),
        name="in_proj_" + kind,
    )(u, w, *extras, *r_arrays)
    return outs[0], outs[1:]


FLIP_TILE = 256


def _dft_constants(seq):
    gd = FOURIER_GROUP_DIM
    half = seq // 2
    jc = np.arange(gd, dtype=np.int64)
    ang_c = 2.0 * np.pi * ((jc[:, None] * jc[None, :]) % gd) / gd
    chan = np.concatenate([np.cos(ang_c), np.sin(ang_c)], axis=1) / math.sqrt(gd)
    j = np.arange(half + BF16_SUBLANES, dtype=np.int64)[:, None]
    n = np.arange(half, dtype=np.int64)[None, :]
    ang_s = 2.0 * np.pi * ((j * n) % seq) / seq
    cos_m = np.cos(ang_s) / math.sqrt(seq)
    cos_m[:, 0] *= 0.5
    cos_m[half + 1:, :] = 0.0
    sin_m = np.sin(ang_s[:half]) / math.sqrt(seq)
    sign = np.where(np.arange(half + BF16_SUBLANES) % 2 == 0, 1.0, -1.0) / math.sqrt(seq)
    sign[half + 1:] = 0.0
    r = np.arange(1, FLIP_TILE)
    exch = np.zeros((FLIP_TILE, FLIP_TILE))
    exch[r, FLIP_TILE - r] = 1.0
    return (jnp.asarray(chan, BF16), jnp.asarray(cos_m, BF16), jnp.asarray(sin_m, BF16),
            jnp.asarray(sign.reshape(-1, 1), F32), jnp.asarray(exch, BF16))


def _flip_tiles(exch, tiles, first_rows):
    row = lax.broadcasted_iota(jnp.int32, tiles[0].shape, 0)
    return [jnp.where(row == 0, first, jnp.dot(exch, tile, preferred_element_type=F32))
            for tile, first in zip(tiles, first_rows)]


def _first_row(x, r):
    return x[r:r + BF16_SUBLANES, :].astype(F32)[0:1, :]


def _fourier_kernel(*refs, n_riders):
    f_ref, chan_ref, cos_ref, sin_ref, sign_ref, exch_ref = refs[:6]
    o_ref = refs[6 + n_riders]
    _cast_riders(refs[6:6 + n_riders], refs[7 + n_riders:7 + 2 * n_riders])
    seq, gd = f_ref.shape
    half = seq // 2
    nt = half // FLIP_TILE
    exch = exch_ref[...]

    rev = jnp.concatenate(_flip_tiles(
        exch,
        [f_ref[seq - (t + 1) * FLIP_TILE:seq - t * FLIP_TILE, :] for t in range(nt)],
        [_first_row(f_ref, 0 if t == 0 else seq - t * FLIP_TILE) for t in range(nt)]), axis=0)
    x_lo = f_ref[0:half, :].astype(F32)
    even = (x_lo + rev).astype(BF16)
    odd = (x_lo - rev).astype(BF16)
    p_even = jnp.dot(even, chan_ref[:, 0:gd], preferred_element_type=F32).astype(BF16)
    q_odd = jnp.dot(odd, chan_ref[:, gd:2 * gd], preferred_element_type=F32).astype(BF16)
    p_mid = jnp.dot(f_ref[half:half + BF16_SUBLANES, :], chan_ref[:, 0:gd],
                    preferred_element_type=F32)[0:1, :]
    a = jnp.dot(cos_ref[...], p_even, preferred_element_type=F32) + sign_ref[...] * p_mid
    b = jnp.dot(sin_ref[...], q_odd, preferred_element_type=F32)
    o_ref[0:half, :] = (a[0:half] - b).astype(o_ref.dtype)

    z = (a[0:half] + b).astype(BF16)
    upper = _flip_tiles(
        exch,
        [z[half - (u + 1) * FLIP_TILE:half - u * FLIP_TILE, :] for u in range(nt)],
        [a[half:half + 1, :] if u == 0 else _first_row(z, half - u * FLIP_TILE) for u in range(nt)])
    for u in range(nt):
        o_ref[half + u * FLIP_TILE:half + (u + 1) * FLIP_TILE, :] = upper[u].astype(o_ref.dtype)


def _fourier_mix(f, batch, seq, consts, riders):
    t, width = f.shape
    gd = FOURIER_GROUP_DIM
    groups = width // gd
    r_arrays, r_in, r_shapes, r_out = _rider_specs(riders, batch * groups, groups)
    outs = pl.pallas_call(
        functools.partial(_fourier_kernel, n_riders=len(riders)),
        out_shape=[jax.ShapeDtypeStruct((t, width), BF16)] + r_shapes,
        grid=(batch, groups),
        in_specs=[pl.BlockSpec((seq, gd), lambda b, g: (b, g))]
                 + [pl.BlockSpec(c.shape, lambda b, g: (0, 0)) for c in consts] + r_in,
        out_specs=[pl.BlockSpec((seq, gd), lambda b, g: (b, g))] + r_out,
        compiler_params=_params(("arbitrary", "arbitrary")),
        name="fourier_mix",
    )(f, *consts, *r_arrays)
    return outs[0], outs[1:]


def _attn_kernel(*refs, lam_init, tq, n_riders):
    lam_ref, g_ref, q_ref, k_ref, v_ref = refs[:5]
    o_ref = refs[5 + n_riders]
    s0, s1, p0, p1, l0, l1 = refs[6 + 2 * n_riders:]
    _cast_riders(refs[5:5 + n_riders], refs[6 + n_riders:6 + 2 * n_riders])
    nq = q_ref.shape[0] // tq
    s_buf, p_buf, l_buf = (s0, s1), (p0, p1), (l0, l1)
    lp = lam_ref[...]
    lam = (jnp.exp(jnp.sum(lp[0:1] * lp[1:2], axis=-1, keepdims=True))
           - jnp.exp(jnp.sum(lp[2:3] * lp[3:4], axis=-1, keepdims=True)) + lam_init)

    def scores(j, slot):
        q = q_ref[j * tq:(j + 1) * tq, :]
        for c in range(2):
            sl = slice(c * HEAD_DIM, (c + 1) * HEAD_DIM)
            s_buf[slot][c] = lax.dot_general(q[:, sl], k_ref[:, sl], (((1,), (1,)), ((), ())),
                                             preferred_element_type=F32)

    def softmax(slot):
        for c in range(2):
            for r in range(0, tq, BF16_SUBLANES):
                rows = slice(r, r + BF16_SUBLANES)
                s = s_buf[slot][c, rows, :]
                p = jnp.exp2(s - jnp.max(s, axis=-1, keepdims=True))
                l_buf[slot][c, rows, :] = jnp.broadcast_to(jnp.sum(p, axis=-1, keepdims=True),
                                                           (BF16_SUBLANES, LANES))
                p_buf[slot][c, rows, :] = p.astype(BF16)

    def values(j, slot):
        pv = jnp.dot(p_buf[slot][...].reshape(2 * tq, -1), v_ref[...], preferred_element_type=F32)
        o = [pv[c * tq:(c + 1) * tq] / jnp.concatenate([l_buf[slot][c]] * (v_ref.shape[1] // LANES), axis=1)
             for c in range(2)]
        o = o[0] - lam * o[1]
        ms = jnp.mean(o * o, axis=-1, keepdims=True)
        o = o * lax.rsqrt(ms + RMS_EPS) * g_ref[...] * (1.0 - lam_init)
        o_ref[j * tq:(j + 1) * tq, :] = o.astype(o_ref.dtype)

    scores(0, 0)
    for j in range(nq):
        if j + 1 < nq:
            scores(j + 1, (j + 1) % 2)
        softmax(j % 2)
        if j >= 1:
            values(j - 1, (j - 1) % 2)
    values(nq - 1, (nq - 1) % 2)


def _diff_attention(qk, v, lam_params, sub_g, batch, seq, lam_init, riders, tq=128):
    t = qk.shape[0]
    hw = 2 * HEAD_DIM
    r_arrays, r_in, r_shapes, r_out = _rider_specs(riders, batch * HEADS, HEADS)
    outs = pl.pallas_call(
        functools.partial(_attn_kernel, lam_init=lam_init, tq=tq, n_riders=len(riders)),
        out_shape=[jax.ShapeDtypeStruct((t, ATTN_WIDTH), BF16)] + r_shapes,
        grid=(batch, HEADS),
        in_specs=[pl.BlockSpec((4, HEAD_DIM), lambda b, h: (0, 0)),
                  pl.BlockSpec((1, hw), lambda b, h: (0, 0)),
                  pl.BlockSpec((seq, hw), lambda b, h: (b, h)),
                  pl.BlockSpec((seq, hw), lambda b, h: (b, HEADS + h)),
                  pl.BlockSpec((seq, hw), lambda b, h: (b, h))] + r_in,
        out_specs=[pl.BlockSpec((seq, hw), lambda b, h: (b, h))] + r_out,
        scratch_shapes=[pltpu.VMEM((2, tq, seq), F32), pltpu.VMEM((2, tq, seq), F32),
                        pltpu.VMEM((2, tq, seq), BF16), pltpu.VMEM((2, tq, seq), BF16),
                        pltpu.VMEM((2, tq, LANES), F32), pltpu.VMEM((2, tq, LANES), F32)],
        compiler_params=_params(("arbitrary", "arbitrary")),
        name="diff_attention",
    )(lam_params, sub_g, qk, qk, v, *r_arrays)
    return outs[0], outs[1:]


def _merge_kernel(yf_ref, o_ref_in, wf_ref, wa_ref, gf_ref, ga_ref, out_ref):
    y_f = jnp.dot(yf_ref[...], wf_ref[...], preferred_element_type=F32)
    y_a = jnp.dot(o_ref_in[...], wa_ref[...], preferred_element_type=F32)
    out_ref[...] = (gf_ref[...].astype(F32) * y_f + ga_ref[...].astype(F32) * y_a).astype(out_ref.dtype)


def _merge(yf, oa, wf, wa, gates, bm=1024, bn=512):
    t, kf = yf.shape
    ka = oa.shape[1]
    d = wf.shape[1]
    ga_off = d // bn
    return pl.pallas_call(
        _merge_kernel,
        out_shape=jax.ShapeDtypeStruct((t, d), BF16),
        grid=(t // bm, d // bn),
        in_specs=[pl.BlockSpec((bm, kf), lambda i, j: (i, 0)),
                  pl.BlockSpec((bm, ka), lambda i, j: (i, 0)),
                  pl.BlockSpec((kf, bn), lambda i, j: (0, j)),
                  pl.BlockSpec((ka, bn), lambda i, j: (0, j)),
                  pl.BlockSpec((bm, bn), lambda i, j: (i, j)),
                  pl.BlockSpec((bm, bn), lambda i, j: (i, j + ga_off))],
        out_specs=pl.BlockSpec((bm, bn), lambda i, j: (i, j)),
        compiler_params=_params(("arbitrary", "arbitrary")),
        name="merge_proj",
    )(yf, oa, wf, wa, gates, gates)


def _mm_residual_kernel(a_ref, w_ref, r_ref, o_ref):
    acc = jnp.dot(a_ref[...], w_ref[...], preferred_element_type=F32)
    o_ref[...] = r_ref[...] + acc


def _mm_residual(a, w, res, bm=512, bn=512):
    t, k = a.shape
    n = w.shape[1]
    return pl.pallas_call(
        _mm_residual_kernel,
        out_shape=jax.ShapeDtypeStruct((t, n), F32),
        grid=(t // bm, n // bn),
        in_specs=[pl.BlockSpec((bm, k), lambda i, j: (i, 0)),
                  pl.BlockSpec((k, bn), lambda i, j: (0, j)),
                  pl.BlockSpec((bm, bn), lambda i, j: (i, j))],
        out_specs=pl.BlockSpec((bm, bn), lambda i, j: (i, j)),
        compiler_params=_params(("arbitrary", "arbitrary")),
        name="mm_residual",
    )(a, w, res)


def _mm_residual_stats_kernel(a_ref, w_ref, r_ref, g_ref, h_ref, hg_ref, ssq_ref):
    h = r_ref[...] + jnp.dot(a_ref[...], w_ref[...], preferred_element_type=F32)
    h_ref[...] = h
    hg_ref[...] = (h * g_ref[...]).astype(hg_ref.dtype)

    @pl.when(pl.program_id(1) == 0)
    def _():
        ssq_ref[...] = jnp.zeros_like(ssq_ref)

    ssq_ref[...] += jnp.sum(h * h, axis=-1, keepdims=True)


def _mm_residual_stats(a, w, res, g_row, bm=1024, bn=512):
    t, k = a.shape
    n = w.shape[1]
    return pl.pallas_call(
        _mm_residual_stats_kernel,
        out_shape=(jax.ShapeDtypeStruct((t, n), F32),
                   jax.ShapeDtypeStruct((t, n), BF16),
                   jax.ShapeDtypeStruct((t, 1), F32)),
        grid=(t // bm, n // bn),
        in_specs=[pl.BlockSpec((bm, k), lambda i, j: (i, 0)),
                  pl.BlockSpec((k, bn), lambda i, j: (0, j)),
                  pl.BlockSpec((bm, bn), lambda i, j: (i, j)),
                  pl.BlockSpec((1, bn), lambda i, j: (0, j))],
        out_specs=(pl.BlockSpec((bm, bn), lambda i, j: (i, j)),
                   pl.BlockSpec((bm, bn), lambda i, j: (i, j)),
                   pl.BlockSpec((bm, 1), lambda i, j: (i, 0))),
        compiler_params=_params(("arbitrary", "arbitrary")),
        name="mm_residual_stats",
    )(a, w, res, g_row)


def _swiglu_kernel(hg_ref, ssq_ref, wg_ref, wu_ref, o_ref):
    hg = hg_ref[...]
    rstd = lax.rsqrt(ssq_ref[...] * (1.0 / hg.shape[1]) + RMS_EPS)
    gate = jnp.dot(hg, wg_ref[...], preferred_element_type=F32) * rstd
    up = jnp.dot(hg, wu_ref[...], preferred_element_type=F32) * rstd
    o_ref[...] = (gate * _sigmoid(gate) * up).astype(o_ref.dtype)


def _swiglu(hg, ssq, wg, wu, bm=2048, bn=256):
    t, k = hg.shape
    n = wg.shape[1]
    return pl.pallas_call(
        _swiglu_kernel,
        out_shape=jax.ShapeDtypeStruct((t, n), BF16),
        grid=(t // bm, n // bn),
        in_specs=[pl.BlockSpec((bm, k), lambda i, j: (i, 0)),
                  pl.BlockSpec((bm, 1), lambda i, j: (i, 0)),
                  pl.BlockSpec((k, bn), lambda i, j: (0, j)),
                  pl.BlockSpec((k, bn), lambda i, j: (0, j))],
        out_specs=pl.BlockSpec((bm, bn), lambda i, j: (i, j)),
        compiler_params=_params(("arbitrary", "arbitrary")),
        name="swiglu_up",
    )(hg, ssq, wg, wu)


def kernel(x, positions, norm_mix_g, w_in, b_gate, lambda_q1, lambda_k1, lambda_q2, lambda_k2,
           subln_g, w_fourier_out, w_attn_out, w_out, norm_ffn_g, w_ffn_gate, w_ffn_up,
           w_ffn_down, norm_final_g):
    batch, seq, d = x.shape
    depth = w_in.shape[0]
    t = batch * seq
    h = x.reshape(t, d)

    inv_freq = ROPE_THETA ** (-jnp.arange(0, HEAD_DIM, 2, dtype=F32) / HEAD_DIM)
    invf_full = jnp.concatenate([inv_freq, inv_freq]).reshape(1, HEAD_DIM)
    pos_col = positions.reshape(t, 1)
    qk_scale = jnp.concatenate([jnp.full((1, ATTN_WIDTH), math.log2(math.e) / math.sqrt(HEAD_DIM), F32),
                                jnp.ones((1, ATTN_WIDTH), F32)], axis=1)
    dft_consts = _dft_constants(seq)

    for l in range(depth):
        lam_init = 0.8 - 0.6 * math.exp(-0.3 * l)
        mix_cols = FOURIER_WIDTH + 3 * ATTN_WIDTH
        w_mix_b = w_in[l][:, :mix_cols].astype(BF16)
        if l == 0:
            u, cos_t, sin_t = _rmsnorm(h, norm_mix_g[l].reshape(1, d), BF16, rope=(pos_col, invf_full))
        else:
            u = _rmsnorm(h, norm_mix_g[l].reshape(1, d), BF16)
        f, _ = _in_proj(u, w_mix_b, 0, FOURIER_WIDTH, "cast")
        qk, (w_gates_b,) = _in_proj(u, w_mix_b, FOURIER_WIDTH, 2 * ATTN_WIDTH, "rope",
                                    (cos_t, sin_t, qk_scale), riders=[(w_in[l], mix_cols, 2 * d)])
        v, _ = _in_proj(u, w_mix_b, FOURIER_WIDTH + 2 * ATTN_WIDTH, ATTN_WIDTH, "cast")
        gates, (w_down_b,) = _in_proj(u, w_gates_b, 0, 2 * d, "sigmoid", (b_gate[l].reshape(1, 2 * d),),
                                      riders=[w_ffn_down[l]])

        y_f, (w_f_b, w_a_b) = _fourier_mix(
            f, batch, seq, dft_consts, [w_fourier_out[l], w_attn_out[l]])
        lam_params = jnp.stack([lambda_q1[l], lambda_k1[l], lambda_q2[l], lambda_k2[l]]).astype(F32)
        o_a, (w_gate_b, w_up_b, w_out_b) = _diff_attention(
            qk, v, lam_params, subln_g[l].reshape(1, 2 * HEAD_DIM).astype(F32), batch, seq, lam_init,
            [w_ffn_gate[l], w_ffn_up[l], w_out[l]])

        merged = _merge(y_f, o_a, w_f_b, w_a_b, gates)
        h, hg, ssq = _mm_residual_stats(merged, w_out_b, h, norm_ffn_g[l].reshape(1, d).astype(F32))
        hid = _swiglu(hg, ssq, w_gate_b, w_up_b)
        h = _mm_residual(hid, w_down_b, h)

    out = _rmsnorm(h, norm_final_g.reshape(1, d), x.dtype)
    return out.reshape(batch, seq, d)
```

```python
import functools
import math

import numpy as np
import jax
import jax.numpy as jnp
from jax import lax
from jax.experimental import pallas as pl
from jax.experimental.pallas import tpu as pltpu

F32 = jnp.float32
BF16 = jnp.bfloat16

D_MODEL = 4096
FOURIER_WIDTH = D_MODEL // 2
FOURIER_GROUPS = 8
FOURIER_GROUP_DIM = FOURIER_WIDTH // FOURIER_GROUPS
HEAD_DIM = 128
HEADS = (D_MODEL // 2) // (2 * HEAD_DIM)
ATTN_WIDTH = HEADS * 2 * HEAD_DIM
FFN_HIDDEN = -(-8 * D_MODEL // (3 * 256)) * 256
ROPE_THETA = 10000.0
RMS_EPS = 1e-6

LANES = 128
BF16_SUBLANES = 16
VMEM_LIMIT = 56 * 1024 * 1024
VMEM_LIMIT_WIDE = 60 * 1024 * 1024


def _params(sem, vmem_limit=VMEM_LIMIT):
    return pltpu.CompilerParams(dimension_semantics=sem, vmem_limit_bytes=vmem_limit)


def _sigmoid(x):
    return 0.5 * jnp.tanh(0.5 * x) + 0.5


def _rider_rows(rows, steps):
    for nblk in range(steps, 0, -1):
        if rows % nblk == 0 and (rows // nblk) % BF16_SUBLANES == 0:
            return rows // nblk
    raise ValueError(f"cannot split {rows} rows into at most {steps} bf16-tileable blocks")


def _rider_specs(riders, steps, inner):
    arrays, in_specs, out_shapes, out_specs = [], [], [], []
    for rider in riders:
        w, col0, ncols = rider if isinstance(rider, tuple) else (rider, 0, rider.shape[1])
        assert col0 % ncols == 0
        rb = _rider_rows(w.shape[0], steps)
        last = w.shape[0] // rb - 1
        in_index = lambda a, b, last=last, cb=col0 // ncols: (jnp.minimum(a * inner + b, last), cb)
        out_index = lambda a, b, last=last: (jnp.minimum(a * inner + b, last), 0)
        arrays.append(w)
        in_specs.append(pl.BlockSpec((rb, ncols), in_index))
        out_specs.append(pl.BlockSpec((rb, ncols), out_index))
        out_shapes.append(jax.ShapeDtypeStruct((w.shape[0], ncols), BF16))
    return arrays, in_specs, out_shapes, out_specs


def _cast_riders(rider_in, rider_out):
    for w_ref, o_ref in zip(rider_in, rider_out):
        o_ref[...] = w_ref[...].astype(o_ref.dtype)


def _rope_tables(pos_ref, invf_ref, cos_ref, sin_ref):
    ang = pos_ref[...].astype(F32) * invf_ref[...]
    lane = lax.broadcasted_iota(jnp.int32, ang.shape, 1)
    s = jnp.sin(ang)
    cos_ref[...] = jnp.cos(ang)
    sin_ref[...] = jnp.where(lane < HEAD_DIM // 2, -s, s)


def _rmsnorm_kernel(x_ref, g_ref, o_ref):
    x = x_ref[...].astype(F32)
    ms = jnp.mean(x * x, axis=-1, keepdims=True)
    o_ref[...] = (x * lax.rsqrt(ms + RMS_EPS) * g_ref[...]).astype(o_ref.dtype)


def _rmsnorm_rope_kernel(x_ref, g_ref, pos_ref, invf_ref, o_ref, cos_ref, sin_ref):
    _rmsnorm_kernel(x_ref, g_ref, o_ref)
    _rope_tables(pos_ref, invf_ref, cos_ref, sin_ref)


def _rmsnorm(x, g_row, out_dtype, rope=None, tm=512):
    t, d = x.shape
    row = lambda i: (i, 0)
    fixed = lambda i: (0, 0)
    in_specs = [pl.BlockSpec((tm, d), row), pl.BlockSpec((1, d), fixed)]
    out_shape = jax.ShapeDtypeStruct((t, d), out_dtype)
    out_specs = pl.BlockSpec((tm, d), row)
    if rope is None:
        body, args = _rmsnorm_kernel, (x, g_row)
    else:
        body, args = _rmsnorm_rope_kernel, (x, g_row, *rope)
        in_specs += [pl.BlockSpec((tm, 1), row), pl.BlockSpec((1, HEAD_DIM), fixed)]
        out_shape = (out_shape,) + (jax.ShapeDtypeStruct((t, HEAD_DIM), F32),) * 2
        out_specs = (out_specs,) + (pl.BlockSpec((tm, HEAD_DIM), row),) * 2
    return pl.pallas_call(
        body,
        out_shape=out_shape,
        grid=(t // tm,),
        in_specs=in_specs,
        out_specs=out_specs,
        compiler_params=_params(("arbitrary",)),
        name="rmsnorm",
    )(*args)


IN_PROJ_EXTRAS = {"cast": 0, "rope": 3, "sigmoid": 1}


def _in_proj_kernel(*refs, kind, n_riders):
    n_in = 2 + IN_PROJ_EXTRAS[kind]
    a_ref, w_ref = refs[:2]
    o_ref = refs[n_in + n_riders]
    _cast_riders(refs[n_in:n_in + n_riders], refs[n_in + n_riders + 1:])
    acc = jnp.dot(a_ref[...], w_ref[...], preferred_element_type=F32)
    if kind == "cast":
        o_ref[...] = acc.astype(o_ref.dtype)
    elif kind == "rope":
        cos_ref, sin_ref, cs_ref = refs[2:n_in]
        cos = cos_ref[...]
        sin = sin_ref[...]
        for c in range(acc.shape[1] // HEAD_DIM):
            sl = slice(c * HEAD_DIM, (c + 1) * HEAD_DIM)
            xc = acc[:, sl]
            r = xc * cos + pltpu.roll(xc, HEAD_DIM // 2, 1) * sin
            o_ref[:, sl] = (r * cs_ref[:, sl]).astype(o_ref.dtype)
    else:
        o_ref[...] = _sigmoid(acc + refs[2][...]).astype(o_ref.dtype)


def _in_proj(u, w, col0, ncols, kind, extras=(), riders=(), bm=1024, bn=1024):
    t, k = u.shape
    joff = col0 // bn
    grid = (t // bm, ncols // bn)
    in_specs = [pl.BlockSpec((bm, k), lambda i, j: (i, 0)),
                pl.BlockSpec((k, bn), lambda i, j: (0, j + joff))]
    if kind == "rope":
        in_specs += [pl.BlockSpec((bm, HEAD_DIM), lambda i, j: (i, 0)),
                     pl.BlockSpec((bm, HEAD_DIM), lambda i, j: (i, 0)),
                     pl.BlockSpec((1, bn), lambda i, j: (0, j))]
    elif kind == "sigmoid":
        in_specs += [pl.BlockSpec((1, bn), lambda i, j: (0, j))]
    assert len(extras) == IN_PROJ_EXTRAS[kind]
    r_arrays, r_in, r_shapes, r_out = _rider_specs(riders, grid[0] * grid[1], grid[1])
    outs = pl.pallas_call(
        functools.partial(_in_proj_kernel, kind=kind, n_riders=len(riders)),
        out_shape=[jax.ShapeDtypeStruct((t, ncols), BF16)] + r_shapes,
        grid=grid,
        in_specs=in_specs + r_in,
        out_specs=[pl.BlockSpec((bm, bn), lambda i, j: (i, j))] + r_out,
        compiler_params=_params(("arbitrary", "arbitrary")),
        name="in_proj_" + kind,
    )(u, w, *extras, *r_arrays)
    return outs[0], outs[1:]


FLIP_TILE = 256


def _dft_constants(seq):
    gd = FOURIER_GROUP_DIM
    half = seq // 2
    jc = np.arange(gd, dtype=np.int64)
    ang_c = 2.0 * np.pi * ((jc[:, None] * jc[None, :]) % gd) / gd
    chan = np.concatenate([np.cos(ang_c), np.sin(ang_c)], axis=1) / math.sqrt(gd)
    j = np.arange(half + BF16_SUBLANES, dtype=np.int64)[:, None]
    n = np.arange(half, dtype=np.int64)[None, :]
    ang_s = 2.0 * np.pi * ((j * n) % seq) / seq
    cos_m = np.cos(ang_s) / math.sqrt(seq)
    cos_m[:, 0] *= 0.5
    cos_m[half + 1:, :] = 0.0
    sin_m = np.sin(ang_s[:half]) / math.sqrt(seq)
    sign = np.where(np.arange(half + BF16_SUBLANES) % 2 == 0, 1.0, -1.0) / math.sqrt(seq)
    sign[half + 1:] = 0.0
    r = np.arange(1, FLIP_TILE)
    exch = np.zeros((FLIP_TILE, FLIP_TILE))
    exch[r, FLIP_TILE - r] = 1.0
    return (jnp.asarray(chan, BF16), jnp.asarray(cos_m, BF16), jnp.asarray(sin_m, BF16),
            jnp.asarray(sign.reshape(-1, 1), F32), jnp.asarray(exch, BF16))


def _flip_tiles(exch, tiles, first_rows):
    row = lax.broadcasted_iota(jnp.int32, tiles[0].shape, 0)
    return [jnp.where(row == 0, first, jnp.dot(exch, tile, preferred_element_type=F32))
            for tile, first in zip(tiles, first_rows)]


def _first_row(x, r):
    return x[r:r + BF16_SUBLANES, :].astype(F32)[0:1, :]


def _fourier_kernel(*refs, n_riders):
    f_ref, chan_ref, cos_ref, sin_ref, sign_ref, exch_ref = refs[:6]
    o_ref = refs[6 + n_riders]
    _cast_riders(refs[6:6 + n_riders], refs[7 + n_riders:7 + 2 * n_riders])
    seq, gd = f_ref.shape
    half = seq // 2
    nt = half // FLIP_TILE
    exch = exch_ref[...]

    rev = jnp.concatenate(_flip_tiles(
        exch,
        [f_ref[seq - (t + 1) * FLIP_TILE:seq - t * FLIP_TILE, :] for t in range(nt)],
        [_first_row(f_ref, 0 if t == 0 else seq - t * FLIP_TILE) for t in range(nt)]), axis=0)
    x_lo = f_ref[0:half, :].astype(F32)
    even = (x_lo + rev).astype(BF16)
    odd = (x_lo - rev).astype(BF16)
    p_even = jnp.dot(even, chan_ref[:, 0:gd], preferred_element_type=F32).astype(BF16)
    q_odd = jnp.dot(odd, chan_ref[:, gd:2 * gd], preferred_element_type=F32).astype(BF16)
    p_mid = jnp.dot(f_ref[half:half + BF16_SUBLANES, :], chan_ref[:, 0:gd],
                    preferred_element_type=F32)[0:1, :]
    a = jnp.dot(cos_ref[...], p_even, preferred_element_type=F32) + sign_ref[...] * p_mid
    b = jnp.dot(sin_ref[...], q_odd, preferred_element_type=F32)
    o_ref[0:half, :] = (a[0:half] - b).astype(o_ref.dtype)

    z = (a[0:half] + b).astype(BF16)
    upper = _flip_tiles(
        exch,
        [z[half - (u + 1) * FLIP_TILE:half - u * FLIP_TILE, :] for u in range(nt)],
        [a[half:half + 1, :] if u == 0 else _first_row(z, half - u * FLIP_TILE) for u in range(nt)])
    for u in range(nt):
        o_ref[half + u * FLIP_TILE:half + (u + 1) * FLIP_TILE, :] = upper[u].astype(o_ref.dtype)


def _fourier_mix(f, batch, seq, consts, riders):
    t, width = f.shape
    gd = FOURIER_GROUP_DIM
    groups = width // gd
    r_arrays, r_in, r_shapes, r_out = _rider_specs(riders, batch * groups, groups)
    outs = pl.pallas_call(
        functools.partial(_fourier_kernel, n_riders=len(riders)),
        out_shape=[jax.ShapeDtypeStruct((t, width), BF16)] + r_shapes,
        grid=(batch, groups),
        in_specs=[pl.BlockSpec((seq, gd), lambda b, g: (b, g))]
                 + [pl.BlockSpec(c.shape, lambda b, g: (0, 0)) for c in consts] + r_in,
        out_specs=[pl.BlockSpec((seq, gd), lambda b, g: (b, g))] + r_out,
        compiler_params=_params(("arbitrary", "arbitrary")),
        name="fourier_mix",
    )(f, *consts, *r_arrays)
    return outs[0], outs[1:]


def _attn_kernel(*refs, lam_init, tq, n_riders):
    lam_ref, g_ref, q_ref, k_ref, v_ref = refs[:5]
    o_ref = refs[5 + n_riders]
    s0, s1, p0, p1, l0, l1 = refs[6 + 2 * n_riders:]
    _cast_riders(refs[5:5 + n_riders], refs[6 + n_riders:6 + 2 * n_riders])
    nq = q_ref.shape[0] // tq
    s_buf, p_buf, l_buf = (s0, s1), (p0, p1), (l0, l1)
    lp = lam_ref[...]
    lam = (jnp.exp(jnp.sum(lp[0:1] * lp[1:2], axis=-1, keepdims=True))
           - jnp.exp(jnp.sum(lp[2:3] * lp[3:4], axis=-1, keepdims=True)) + lam_init)

    def scores(j, slot):
        q = q_ref[j * tq:(j + 1) * tq, :]
        for c in range(2):
            sl = slice(c * HEAD_DIM, (c + 1) * HEAD_DIM)
            s_buf[slot][c] = lax.dot_general(q[:, sl], k_ref[:, sl], (((1,), (1,)), ((), ())),
                                             preferred_element_type=F32)

    def softmax(slot):
        for c in range(2):
            for r in range(0, tq, BF16_SUBLANES):
                rows = slice(r, r + BF16_SUBLANES)
                s = s_buf[slot][c, rows, :]
                p = jnp.exp2(s - jnp.max(s, axis=-1, keepdims=True))
                l_buf[slot][c, rows, :] = jnp.broadcast_to(jnp.sum(p, axis=-1, keepdims=True),
                                                           (BF16_SUBLANES, LANES))
                p_buf[slot][c, rows, :] = p.astype(BF16)

    def values(j, slot):
        pv = jnp.dot(p_buf[slot][...].reshape(2 * tq, -1), v_ref[...], preferred_element_type=F32)
        o = [pv[c * tq:(c + 1) * tq] / jnp.concatenate([l_buf[slot][c]] * (v_ref.shape[1] // LANES), axis=1)
             for c in range(2)]
        o = o[0] - lam * o[1]
        ms = jnp.mean(o * o, axis=-1, keepdims=True)
        o = o * lax.rsqrt(ms + RMS_EPS) * g_ref[...] * (1.0 - lam_init)
        o_ref[j * tq:(j + 1) * tq, :] = o.astype(o_ref.dtype)

    scores(0, 0)
    for j in range(nq):
        if j + 1 < nq:
            scores(j + 1, (j + 1) % 2)
        softmax(j % 2)
        if j >= 1:
            values(j - 1, (j - 1) % 2)
    values(nq - 1, (nq - 1) % 2)


def _diff_attention(qk, v, lam_params, sub_g, batch, seq, lam_init, riders, tq=128):
    t = qk.shape[0]
    hw = 2 * HEAD_DIM
    r_arrays, r_in, r_shapes, r_out = _rider_specs(riders, batch * HEADS, HEADS)
    outs = pl.pallas_call(
        functools.partial(_attn_kernel, lam_init=lam_init, tq=tq, n_riders=len(riders)),
        out_shape=[jax.ShapeDtypeStruct((t, ATTN_WIDTH), BF16)] + r_shapes,
        grid=(batch, HEADS),
        in_specs=[pl.BlockSpec((4, HEAD_DIM), lambda b, h: (0, 0)),
                  pl.BlockSpec((1, hw), lambda b, h: (0, 0)),
                  pl.BlockSpec((seq, hw), lambda b, h: (b, h)),
                  pl.BlockSpec((seq, hw), lambda b, h: (b, HEADS + h)),
                  pl.BlockSpec((seq, hw), lambda b, h: (b, h))] + r_in,
        out_specs=[pl.BlockSpec((seq, hw), lambda b, h: (b, h))] + r_out,
        scratch_shapes=[pltpu.VMEM((2, tq, seq), F32), pltpu.VMEM((2, tq, seq), F32),
                        pltpu.VMEM((2, tq, seq), BF16), pltpu.VMEM((2, tq, seq), BF16),
                        pltpu.VMEM((2, tq, LANES), F32), pltpu.VMEM((2, tq, LANES), F32)],
        compiler_params=_params(("arbitrary", "arbitrary")),
        name="diff_attention",
    )(lam_params, sub_g, qk, qk, v, *r_arrays)
    return outs[0], outs[1:]


def _merge_kernel(yf_ref, o_ref_in, wf_ref, wa_ref, gf_ref, ga_ref, out_ref):
    y_f = jnp.dot(yf_ref[...], wf_ref[...], preferred_element_type=F32)
    y_a = jnp.dot(o_ref_in[...], wa_ref[...], preferred_element_type=F32)
    out_ref[...] = (gf_ref[...].astype(F32) * y_f + ga_ref[...].astype(F32) * y_a).astype(out_ref.dtype)


def _merge(yf, oa, wf, wa, gates, bm=1024, bn=512):
    t, kf = yf.shape
    ka = oa.shape[1]
    d = wf.shape[1]
    ga_off = d // bn
    return pl.pallas_call(
        _merge_kernel,
        out_shape=jax.ShapeDtypeStruct((t, d), BF16),
        grid=(t // bm, d // bn),
        in_specs=[pl.BlockSpec((bm, kf), lambda i, j: (i, 0)),
                  pl.BlockSpec((bm, ka), lambda i, j: (i, 0)),
                  pl.BlockSpec((kf, bn), lambda i, j: (0, j)),
                  pl.BlockSpec((ka, bn), lambda i, j: (0, j)),
                  pl.BlockSpec((bm, bn), lambda i, j: (i, j)),
                  pl.BlockSpec((bm, bn), lambda i, j: (i, j + ga_off))],
        out_specs=pl.BlockSpec((bm, bn), lambda i, j: (i, j)),
        compiler_params=_params(("arbitrary", "arbitrary")),
        name="merge_proj",
    )(yf, oa, wf, wa, gates, gates)


def _mm_residual_kernel(a_ref, w_ref, r_ref, o_ref):
    acc = jnp.dot(a_ref[...], w_ref[...], preferred_element_type=F32)
    o_ref[...] = r_ref[...] + acc


def _mm_residual(a, w, res, bm=1024, bn=256):
    t, k = a.shape
    n = w.shape[1]
    return pl.pallas_call(
        _mm_residual_kernel,
        out_shape=jax.ShapeDtypeStruct((t, n), F32),
        grid=(t // bm, n // bn),
        in_specs=[pl.BlockSpec((bm, k), lambda i, j: (i, 0)),
                  pl.BlockSpec((k, bn), lambda i, j: (0, j)),
                  pl.BlockSpec((bm, bn), lambda i, j: (i, j))],
        out_specs=pl.BlockSpec((bm, bn), lambda i, j: (i, j)),
        compiler_params=_params(("arbitrary", "arbitrary"), VMEM_LIMIT_WIDE),
        name="mm_residual",
    )(a, w, res)


def _mm_residual_stats_kernel(a_ref, w_ref, r_ref, g_ref, h_ref, hg_ref, ssq_ref):
    h = r_ref[...] + jnp.dot(a_ref[...], w_ref[...], preferred_element_type=F32)
    h_ref[...] = h
    hg_ref[...] = (h * g_ref[...]).astype(hg_ref.dtype)

    @pl.when(pl.program_id(1) == 0)
    def _():
        ssq_ref[...] = jnp.zeros_like(ssq_ref)

    ssq_ref[...] += jnp.sum(h * h, axis=-1, keepdims=True)


def _mm_residual_stats(a, w, res, g_row, bm=1024, bn=512):
    t, k = a.shape
    n = w.shape[1]
    return pl.pallas_call(
        _mm_residual_stats_kernel,
        out_shape=(jax.ShapeDtypeStruct((t, n), F32),
                   jax.ShapeDtypeStruct((t, n), BF16),
                   jax.ShapeDtypeStruct((t, 1), F32)),
        grid=(t // bm, n // bn),
        in_specs=[pl.BlockSpec((bm, k), lambda i, j: (i, 0)),
                  pl.BlockSpec((k, bn), lambda i, j: (0, j)),
                  pl.BlockSpec((bm, bn), lambda i, j: (i, j)),
                  pl.BlockSpec((1, bn), lambda i, j: (0, j))],
        out_specs=(pl.BlockSpec((bm, bn), lambda i, j: (i, j)),
                   pl.BlockSpec((bm, bn), lambda i, j: (i, j)),
                   pl.BlockSpec((bm, 1), lambda i, j: (i, 0))),
        compiler_params=_params(("arbitrary", "arbitrary")),
        name="mm_residual_stats",
    )(a, w, res, g_row)


def _swiglu_kernel(hg_ref, ssq_ref, wg_ref, wu_ref, o_ref):
    hg = hg_ref[...]
    rstd = lax.rsqrt(ssq_ref[...] * (1.0 / hg.shape[1]) + RMS_EPS)
    gate = jnp.dot(hg, wg_ref[...], preferred_element_type=F32) * rstd
    up = jnp.dot(hg, wu_ref[...], preferred_element_type=F32) * rstd
    o_ref[...] = (gate * _sigmoid(gate) * up).astype(o_ref.dtype)


def _swiglu(hg, ssq, wg, wu, bm=2048, bn=256):
    t, k = hg.shape
    n = wg.shape[1]
    return pl.pallas_call(
        _swiglu_kernel,
        out_shape=jax.ShapeDtypeStruct((t, n), BF16),
        grid=(t // bm, n // bn),
        in_specs=[pl.BlockSpec((bm, k), lambda i, j: (i, 0)),
                  pl.BlockSpec((bm, 1), lambda i, j: (i, 0)),
                  pl.BlockSpec((k, bn), lambda i, j: (0, j)),
                  pl.BlockSpec((k, bn), lambda i, j: (0, j))],
        out_specs=pl.BlockSpec((bm, bn), lambda i, j: (i, j)),
        compiler_params=_params(("arbitrary", "arbitrary")),
        name="swiglu_up",
    )(hg, ssq, wg, wu)


def kernel(x, positions, norm_mix_g, w_in, b_gate, lambda_q1, lambda_k1, lambda_q2, lambda_k2,
           subln_g, w_fourier_out, w_attn_out, w_out, norm_ffn_g, w_ffn_gate, w_ffn_up,
           w_ffn_down, norm_final_g):
    batch, seq, d = x.shape
    depth = w_in.shape[0]
    t = batch * seq
    h = x.reshape(t, d)

    inv_freq = ROPE_THETA ** (-jnp.arange(0, HEAD_DIM, 2, dtype=F32) / HEAD_DIM)
    invf_full = jnp.concatenate([inv_freq, inv_freq]).reshape(1, HEAD_DIM)
    pos_col = positions.reshape(t, 1)
    qk_scale = jnp.concatenate([jnp.full((1, ATTN_WIDTH), math.log2(math.e) / math.sqrt(HEAD_DIM), F32),
                                jnp.ones((1, ATTN_WIDTH), F32)], axis=1)
    dft_consts = _dft_constants(seq)

    for l in range(depth):
        lam_init = 0.8 - 0.6 * math.exp(-0.3 * l)
        mix_cols = FOURIER_WIDTH + 3 * ATTN_WIDTH
        w_mix_b = w_in[l][:, :mix_cols].astype(BF16)
        if l == 0:
            u, cos_t, sin_t = _rmsnorm(h, norm_mix_g[l].reshape(1, d), BF16, rope=(pos_col, invf_full))
        else:
            u = _rmsnorm(h, norm_mix_g[l].reshape(1, d), BF16)
        f, _ = _in_proj(u, w_mix_b, 0, FOURIER_WIDTH, "cast")
        qk, (w_gates_b,) = _in_proj(u, w_mix_b, FOURIER_WIDTH, 2 * ATTN_WIDTH, "rope",
                                    (cos_t, sin_t, qk_scale), riders=[(w_in[l], mix_cols, 2 * d)])
        v, _ = _in_proj(u, w_mix_b, FOURIER_WIDTH + 2 * ATTN_WIDTH, ATTN_WIDTH, "cast")
        gates, (w_down_b,) = _in_proj(u, w_gates_b, 0, 2 * d, "sigmoid", (b_gate[l].reshape(1, 2 * d),),
                                      riders=[w_ffn_down[l]])

        y_f, (w_f_b, w_a_b) = _fourier_mix(
            f, batch, seq, dft_consts, [w_fourier_out[l], w_attn_out[l]])
        lam_params = jnp.stack([lambda_q1[l], lambda_k1[l], lambda_q2[l], lambda_k2[l]]).astype(F32)
        o_a, (w_gate_b, w_up_b, w_out_b) = _diff_attention(
            qk, v, lam_params, subln_g[l].reshape(1, 2 * HEAD_DIM).astype(F32), batch, seq, lam_init,
            [w_ffn_gate[l], w_ffn_up[l], w_out[l]])

        merged = _merge(y_f, o_a, w_f_b, w_a_b, gates)
        h, hg, ssq = _mm_residual_stats(merged, w_out_b, h, norm_ffn_g[l].reshape(1, d).astype(F32))
        hid = _swiglu(hg, ssq, w_gate_b, w_up_b)
        h = _mm_residual(hid, w_down_b, h)

    out = _rmsnorm(h, norm_final_g.reshape(1, d), x.dtype)
    return out.reshape(batch, seq, d)
```

```python
import functools
import math

import numpy as np
import jax
import jax.numpy as jnp
from jax import lax
from jax.experimental import pallas as pl
from jax.experimental.pallas import tpu as pltpu

F32 = jnp.float32
BF16 = jnp.bfloat16

D_MODEL = 4096
FOURIER_WIDTH = D_MODEL // 2
FOURIER_GROUPS = 8
FOURIER_GROUP_DIM = FOURIER_WIDTH // FOURIER_GROUPS
HEAD_DIM = 128
HEADS = (D_MODEL // 2) // (2 * HEAD_DIM)
ATTN_WIDTH = HEADS * 2 * HEAD_DIM
ROPE_THETA = 10000.0
RMS_EPS = 1e-6

LANES = 128
BF16_SUBLANES = 16
VMEM_LIMIT = 56 * 1024 * 1024


def _params(sem):
    return pltpu.CompilerParams(dimension_semantics=sem, vmem_limit_bytes=VMEM_LIMIT)


def _sigmoid(x):
    return 0.5 * jnp.tanh(0.5 * x) + 0.5


def _rider_rows(rows, steps):
    for nblk in range(steps, 0, -1):
        if rows % nblk == 0 and (rows // nblk) % BF16_SUBLANES == 0:
            return rows // nblk
    raise ValueError(f"cannot split {rows} rows into at most {steps} bf16-tileable blocks")


def _rider_specs(riders, steps, inner):
    arrays, in_specs, out_shapes, out_specs = [], [], [], []
    for rider in riders:
        w, col0, ncols = rider if isinstance(rider, tuple) else (rider, 0, rider.shape[1])
        assert col0 % ncols == 0
        rb = _rider_rows(w.shape[0], steps)
        last = w.shape[0] // rb - 1
        in_index = lambda a, b, last=last, cb=col0 // ncols: (jnp.minimum(a * inner + b, last), cb)
        out_index = lambda a, b, last=last: (jnp.minimum(a * inner + b, last), 0)
        arrays.append(w)
        in_specs.append(pl.BlockSpec((rb, ncols), in_index))
        out_specs.append(pl.BlockSpec((rb, ncols), out_index))
        out_shapes.append(jax.ShapeDtypeStruct((w.shape[0], ncols), BF16))
    return arrays, in_specs, out_shapes, out_specs


def _cast_riders(rider_in, rider_out):
    for w_ref, o_ref in zip(rider_in, rider_out):
        o_ref[...] = w_ref[...].astype(o_ref.dtype)


def _rope_tables(pos_ref, invf_ref, cos_ref, sin_ref):
    ang = pos_ref[...].astype(F32) * invf_ref[...]
    lane = lax.broadcasted_iota(jnp.int32, ang.shape, 1)
    s = jnp.sin(ang)
    cos_ref[...] = jnp.cos(ang)
    sin_ref[...] = jnp.where(lane < HEAD_DIM // 2, -s, s)


def _rmsnorm_kernel(x_ref, g_ref, o_ref):
    x = x_ref[...].astype(F32)
    ms = jnp.mean(x * x, axis=-1, keepdims=True)
    o_ref[...] = (x * lax.rsqrt(ms + RMS_EPS) * g_ref[...]).astype(o_ref.dtype)


def _rmsnorm_rope_kernel(x_ref, g_ref, pos_ref, invf_ref, o_ref, cos_ref, sin_ref):
    _rmsnorm_kernel(x_ref, g_ref, o_ref)
    _rope_tables(pos_ref, invf_ref, cos_ref, sin_ref)


def _rmsnorm(x, g_row, out_dtype, rope=None, tm=512):
    t, d = x.shape
    row = lambda i: (i, 0)
    fixed = lambda i: (0, 0)
    in_specs = [pl.BlockSpec((tm, d), row), pl.BlockSpec((1, d), fixed)]
    out_shape = jax.ShapeDtypeStruct((t, d), out_dtype)
    out_specs = pl.BlockSpec((tm, d), row)
    if rope is None:
        body, args = _rmsnorm_kernel, (x, g_row)
    else:
        body, args = _rmsnorm_rope_kernel, (x, g_row, *rope)
        in_specs += [pl.BlockSpec((tm, 1), row), pl.BlockSpec((1, HEAD_DIM), fixed)]
        out_shape = (out_shape,) + (jax.ShapeDtypeStruct((t, HEAD_DIM), F32),) * 2
        out_specs = (out_specs,) + (pl.BlockSpec((tm, HEAD_DIM), row),) * 2
    return pl.pallas_call(
        body,
        out_shape=out_shape,
        grid=(t // tm,),
        in_specs=in_specs,
        out_specs=out_specs,
        compiler_params=_params(("arbitrary",)),
        name="rmsnorm",
    )(*args)


IN_PROJ_EXTRAS = {"cast": 0, "rope": 3, "sigmoid": 1}


def _in_proj_kernel(*refs, kind, n_riders):
    n_in = 2 + IN_PROJ_EXTRAS[kind]
    a_ref, w_ref = refs[:2]
    o_ref = refs[n_in + n_riders]
    _cast_riders(refs[n_in:n_in + n_riders], refs[n_in + n_riders + 1:])
    acc = jnp.dot(a_ref[...], w_ref[...], preferred_element_type=F32)
    if kind == "cast":
        o_ref[...] = acc.astype(o_ref.dtype)
    elif kind == "rope":
        cos_ref, sin_ref, cs_ref = refs[2:n_in]
        cos = cos_ref[...]
        sin = sin_ref[...]
        for c in range(acc.shape[1] // HEAD_DIM):
            sl = slice(c * HEAD_DIM, (c + 1) * HEAD_DIM)
            xc = acc[:, sl]
            r = xc * cos + pltpu.roll(xc, HEAD_DIM // 2, 1) * sin
            o_ref[:, sl] = (r * cs_ref[:, sl]).astype(o_ref.dtype)
    else:
        o_ref[...] = _sigmoid(acc + refs[2][...]).astype(o_ref.dtype)


def _in_proj(u, w, col0, ncols, kind, extras=(), riders=(), bm=1024, bn=1024):
    t, k = u.shape
    joff = col0 // bn
    grid = (t // bm, ncols // bn)
    in_specs = [pl.BlockSpec((bm, k), lambda i, j: (i, 0)),
                pl.BlockSpec((k, bn), lambda i, j: (0, j + joff))]
    if kind == "rope":
        in_specs += [pl.BlockSpec((bm, HEAD_DIM), lambda i, j: (i, 0)),
                     pl.BlockSpec((bm, HEAD_DIM), lambda i, j: (i, 0)),
                     pl.BlockSpec((1, bn), lambda i, j: (0, j))]
    elif kind == "sigmoid":
        in_specs += [pl.BlockSpec((1, bn), lambda i, j: (0, j))]
    assert len(extras) == IN_PROJ_EXTRAS[kind]
    r_arrays, r_in, r_shapes, r_out = _rider_specs(riders, grid[0] * grid[1], grid[1])
    outs = pl.pallas_call(
        functools.partial(_in_proj_kernel, kind=kind, n_riders=len(riders)),
        out_shape=[jax.ShapeDtypeStruct((t, ncols), BF16)] + r_shapes,
        grid=grid,
        in_specs=in_specs + r_in,
        out_specs=[pl.BlockSpec((bm, bn), lambda i, j: (i, j))] + r_out,
        compiler_params=_params(("arbitrary", "arbitrary")),
        name="in_proj_" + kind,
    )(u, w, *extras, *r_arrays)
    return outs[0], outs[1:]


FLIP_TILE = 256


def _dft_constants(seq):
    gd = FOURIER_GROUP_DIM
    half = seq // 2
    jc = np.arange(gd, dtype=np.int64)
    ang_c = 2.0 * np.pi * ((jc[:, None] * jc[None, :]) % gd) / gd
    chan = np.concatenate([np.cos(ang_c), np.sin(ang_c)], axis=1) / math.sqrt(gd)
    j = np.arange(half + BF16_SUBLANES, dtype=np.int64)[:, None]
    n = np.arange(half, dtype=np.int64)[None, :]
    ang_s = 2.0 * np.pi * ((j * n) % seq) / seq
    cos_m = np.cos(ang_s) / math.sqrt(seq)
    cos_m[:, 0] *= 0.5
    cos_m[half + 1:, :] = 0.0
    sin_m = np.sin(ang_s[:half]) / math.sqrt(seq)
    sign = np.where(np.arange(half + BF16_SUBLANES) % 2 == 0, 1.0, -1.0) / math.sqrt(seq)
    sign[half + 1:] = 0.0
    r = np.arange(1, FLIP_TILE)
    exch = np.zeros((FLIP_TILE, FLIP_TILE))
    exch[r, FLIP_TILE - r] = 1.0
    return (jnp.asarray(chan, BF16), jnp.asarray(cos_m, BF16), jnp.asarray(sin_m, BF16),
            jnp.asarray(sign.reshape(-1, 1), F32), jnp.asarray(exch, BF16))


def _flip_tiles(exch, tiles, first_rows):
    row = lax.broadcasted_iota(jnp.int32, tiles[0].shape, 0)
    return [jnp.where(row == 0, first, jnp.dot(exch, tile, preferred_element_type=F32))
            for tile, first in zip(tiles, first_rows)]


def _first_row(x, r):
    return x[r:r + BF16_SUBLANES, :].astype(F32)[0:1, :]


def _fourier_kernel(*refs, n_riders):
    f_ref, chan_ref, cos_ref, sin_ref, sign_ref, exch_ref = refs[:6]
    o_ref = refs[6 + n_riders]
    _cast_riders(refs[6:6 + n_riders], refs[7 + n_riders:7 + 2 * n_riders])
    seq, gd = f_ref.shape
    half = seq // 2
    nt = half // FLIP_TILE
    exch = exch_ref[...]

    rev = jnp.concatenate(_flip_tiles(
        exch,
        [f_ref[seq - (t + 1) * FLIP_TILE:seq - t * FLIP_TILE, :] for t in range(nt)],
        [_first_row(f_ref, 0 if t == 0 else seq - t * FLIP_TILE) for t in range(nt)]), axis=0)
    x_lo = f_ref[0:half, :].astype(F32)
    even = (x_lo + rev).astype(BF16)
    odd = (x_lo - rev).astype(BF16)
    p_even = jnp.dot(even, chan_ref[:, 0:gd], preferred_element_type=F32).astype(BF16)
    q_odd = jnp.dot(odd, chan_ref[:, gd:2 * gd], preferred_element_type=F32).astype(BF16)
    p_mid = jnp.dot(f_ref[half:half + BF16_SUBLANES, :], chan_ref[:, 0:gd],
                    preferred_element_type=F32)[0:1, :]
    a = jnp.dot(cos_ref[...], p_even, preferred_element_type=F32) + sign_ref[...] * p_mid
    b = jnp.dot(sin_ref[...], q_odd, preferred_element_type=F32)
    o_ref[0:half, :] = (a[0:half] - b).astype(o_ref.dtype)

    z = (a[0:half] + b).astype(BF16)
    upper = _flip_tiles(
        exch,
        [z[half - (u + 1) * FLIP_TILE:half - u * FLIP_TILE, :] for u in range(nt)],
        [a[half:half + 1, :] if u == 0 else _first_row(z, half - u * FLIP_TILE) for u in range(nt)])
    for u in range(nt):
        o_ref[half + u * FLIP_TILE:half + (u + 1) * FLIP_TILE, :] = upper[u].astype(o_ref.dtype)


def _fourier_mix(f, batch, seq, consts, riders):
    t, width = f.shape
    gd = FOURIER_GROUP_DIM
    groups = width // gd
    r_arrays, r_in, r_shapes, r_out = _rider_specs(riders, batch * groups, groups)
    outs = pl.pallas_call(
        functools.partial(_fourier_kernel, n_riders=len(riders)),
        out_shape=[jax.ShapeDtypeStruct((t, width), BF16)] + r_shapes,
        grid=(batch, groups),
        in_specs=[pl.BlockSpec((seq, gd), lambda b, g: (b, g))]
                 + [pl.BlockSpec(c.shape, lambda b, g: (0, 0)) for c in consts] + r_in,
        out_specs=[pl.BlockSpec((seq, gd), lambda b, g: (b, g))] + r_out,
        compiler_params=_params(("arbitrary", "arbitrary")),
        name="fourier_mix",
    )(f, *consts, *r_arrays)
    return outs[0], outs[1:]


def _attn_kernel(*refs, lam_init, tq, n_riders):
    lam_ref, g_ref, q_ref, k_ref, v_ref = refs[:5]
    o_ref = refs[5 + n_riders]
    s0, s1, p0, p1, l0, l1 = refs[6 + 2 * n_riders:]
    _cast_riders(refs[5:5 + n_riders], refs[6 + n_riders:6 + 2 * n_riders])
    nq = q_ref.shape[0] // tq
    s_buf, p_buf, l_buf = (s0, s1), (p0, p1), (l0, l1)
    lp = lam_ref[...]
    lam = (jnp.exp(jnp.sum(lp[0:1] * lp[1:2], axis=-1, keepdims=True))
           - jnp.exp(jnp.sum(lp[2:3] * lp[3:4], axis=-1, keepdims=True)) + lam_init)

    def scores(j, slot):
        q = q_ref[j * tq:(j + 1) * tq, :]
        for c in range(2):
            sl = slice(c * HEAD_DIM, (c + 1) * HEAD_DIM)
            s_buf[slot][c] = lax.dot_general(q[:, sl], k_ref[:, sl], (((1,), (1,)), ((), ())),
                                             preferred_element_type=F32)

    def softmax(slot):
        for c in range(2):
            for r in range(0, tq, BF16_SUBLANES):
                rows = slice(r, r + BF16_SUBLANES)
                s = s_buf[slot][c, rows, :]
                p = jnp.exp2(s - jnp.max(s, axis=-1, keepdims=True))
                l_buf[slot][c, rows, :] = jnp.broadcast_to(jnp.sum(p, axis=-1, keepdims=True),
                                                           (BF16_SUBLANES, LANES))
                p_buf[slot][c, rows, :] = p.astype(BF16)

    def values(j, slot):
        pv = jnp.dot(p_buf[slot][...].reshape(2 * tq, -1), v_ref[...], preferred_element_type=F32)
        o = [pv[c * tq:(c + 1) * tq] / jnp.concatenate([l_buf[slot][c]] * (v_ref.shape[1] // LANES), axis=1)
             for c in range(2)]
        o = o[0] - lam * o[1]
        ms = jnp.mean(o * o, axis=-1, keepdims=True)
        o = o * lax.rsqrt(ms + RMS_EPS) * g_ref[...] * (1.0 - lam_init)
        o_ref[j * tq:(j + 1) * tq, :] = o.astype(o_ref.dtype)

    scores(0, 0)
    for j in range(nq):
        if j + 1 < nq:
            scores(j + 1, (j + 1) % 2)
        softmax(j % 2)
        if j >= 1:
            values(j - 1, (j - 1) % 2)
    values(nq - 1, (nq - 1) % 2)


def _diff_attention(qk, v, lam_params, sub_g, batch, seq, lam_init, riders, tq=128):
    t = qk.shape[0]
    hw = 2 * HEAD_DIM
    r_arrays, r_in, r_shapes, r_out = _rider_specs(riders, batch * HEADS, HEADS)
    outs = pl.pallas_call(
        functools.partial(_attn_kernel, lam_init=lam_init, tq=tq, n_riders=len(riders)),
        out_shape=[jax.ShapeDtypeStruct((t, ATTN_WIDTH), BF16)] + r_shapes,
        grid=(batch, HEADS),
        in_specs=[pl.BlockSpec((4, HEAD_DIM), lambda b, h: (0, 0)),
                  pl.BlockSpec((1, hw), lambda b, h: (0, 0)),
                  pl.BlockSpec((seq, hw), lambda b, h: (b, h)),
                  pl.BlockSpec((seq, hw), lambda b, h: (b, HEADS + h)),
                  pl.BlockSpec((seq, hw), lambda b, h: (b, h))] + r_in,
        out_specs=[pl.BlockSpec((seq, hw), lambda b, h: (b, h))] + r_out,
        scratch_shapes=[pltpu.VMEM((2, tq, seq), F32), pltpu.VMEM((2, tq, seq), F32),
                        pltpu.VMEM((2, tq, seq), BF16), pltpu.VMEM((2, tq, seq), BF16),
                        pltpu.VMEM((2, tq, LANES), F32), pltpu.VMEM((2, tq, LANES), F32)],
        compiler_params=_params(("arbitrary", "arbitrary")),
        name="diff_attention",
    )(lam_params, sub_g, qk, qk, v, *r_arrays)
    return outs[0], outs[1:]


def _merge_kernel(yf_ref, o_ref_in, wf_ref, wa_ref, gf_ref, ga_ref, out_ref):
    y_f = jnp.dot(yf_ref[...], wf_ref[...], preferred_element_type=F32)
    y_a = jnp.dot(o_ref_in[...], wa_ref[...], preferred_element_type=F32)
    out_ref[...] = (gf_ref[...].astype(F32) * y_f + ga_ref[...].astype(F32) * y_a).astype(out_ref.dtype)


def _merge(yf, oa, wf, wa, gates, bm=1024, bn=512):
    t, kf = yf.shape
    ka = oa.shape[1]
    d = wf.shape[1]
    ga_off = d // bn
    return pl.pallas_call(
        _merge_kernel,
        out_shape=jax.ShapeDtypeStruct((t, d), BF16),
        grid=(t // bm, d // bn),
        in_specs=[pl.BlockSpec((bm, kf), lambda i, j: (i, 0)),
                  pl.BlockSpec((bm, ka), lambda i, j: (i, 0)),
                  pl.BlockSpec((kf, bn), lambda i, j: (0, j)),
                  pl.BlockSpec((ka, bn), lambda i, j: (0, j)),
                  pl.BlockSpec((bm, bn), lambda i, j: (i, j)),
                  pl.BlockSpec((bm, bn), lambda i, j: (i, j + ga_off))],
        out_specs=pl.BlockSpec((bm, bn), lambda i, j: (i, j)),
        compiler_params=_params(("arbitrary", "arbitrary")),
        name="merge_proj",
    )(yf, oa, wf, wa, gates, gates)


def _mm_residual_kernel(a_ref, w_ref, r_ref, o_ref):
    acc = jnp.dot(a_ref[...], w_ref[...], preferred_element_type=F32)
    o_ref[...] = r_ref[...] + acc


def _mm_residual(a, w, res, bm=512, bn=512):
    t, k = a.shape
    n = w.shape[1]
    return pl.pallas_call(
        _mm_residual_kernel,
        out_shape=jax.ShapeDtypeStruct((t, n), F32),
        grid=(t // bm, n // bn),
        in_specs=[pl.BlockSpec((bm, k), lambda i, j: (i, 0)),
                  pl.BlockSpec((k, bn), lambda i, j: (0, j)),
                  pl.BlockSpec((bm, bn), lambda i, j: (i, j))],
        out_specs=pl.BlockSpec((bm, bn), lambda i, j: (i, j)),
        compiler_params=_params(("arbitrary", "arbitrary")),
        name="mm_residual",
    )(a, w, res)


def _mm_residual_stats_kernel(a_ref, w_ref, r_ref, g_ref, h_ref, hg_ref, ssq_ref):
    h = r_ref[...] + jnp.dot(a_ref[...], w_ref[...], preferred_element_type=F32)
    h_ref[...] = h
    hg_ref[...] = (h * g_ref[...]).astype(hg_ref.dtype)

    @pl.when(pl.program_id(1) == 0)
    def _():
        ssq_ref[...] = jnp.zeros_like(ssq_ref)

    ssq_ref[...] += jnp.sum(h * h, axis=-1, keepdims=True)


def _mm_residual_stats(a, w, res, g_row, bm=1024, bn=512):
    t, k = a.shape
    n = w.shape[1]
    return pl.pallas_call(
        _mm_residual_stats_kernel,
        out_shape=(jax.ShapeDtypeStruct((t, n), F32),
                   jax.ShapeDtypeStruct((t, n), BF16),
                   jax.ShapeDtypeStruct((t, 1), F32)),
        grid=(t // bm, n // bn),
        in_specs=[pl.BlockSpec((bm, k), lambda i, j: (i, 0)),
                  pl.BlockSpec((k, bn), lambda i, j: (0, j)),
                  pl.BlockSpec((bm, bn), lambda i, j: (i, j)),
                  pl.BlockSpec((1, bn), lambda i, j: (0, j))],
        out_specs=(pl.BlockSpec((bm, bn), lambda i, j: (i, j)),
                   pl.BlockSpec((bm, bn), lambda i, j: (i, j)),
                   pl.BlockSpec((bm, 1), lambda i, j: (i, 0))),
        compiler_params=_params(("arbitrary", "arbitrary")),
        name="mm_residual_stats",
    )(a, w, res, g_row)


def _swiglu_kernel(hg_ref, ssq_ref, wg_ref, wu_ref, o_ref):
    hg = hg_ref[...]
    rstd = lax.rsqrt(ssq_ref[...] * (1.0 / hg.shape[1]) + RMS_EPS)
    gate = jnp.dot(hg, wg_ref[...], preferred_element_type=F32) * rstd
    up = jnp.dot(hg, wu_ref[...], preferred_element_type=F32) * rstd
    o_ref[...] = (gate * _sigmoid(gate) * up).astype(o_ref.dtype)


def _swiglu(hg, ssq, wg, wu, bm=2048, bn=256):
    t, k = hg.shape
    n = wg.shape[1]
    return pl.pallas_call(
        _swiglu_kernel,
        out_shape=jax.ShapeDtypeStruct((t, n), BF16),
        grid=(t // bm, n // bn),
        in_specs=[pl.BlockSpec((bm, k), lambda i, j: (i, 0)),
                  pl.BlockSpec((bm, 1), lambda i, j: (i, 0)),
                  pl.BlockSpec((k, bn), lambda i, j: (0, j)),
                  pl.BlockSpec((k, bn), lambda i, j: (0, j))],
        out_specs=pl.BlockSpec((bm, bn), lambda i, j: (i, j)),
        compiler_params=_params(("arbitrary", "arbitrary")),
        name="swiglu_up",
    )(hg, ssq, wg, wu)


def kernel(x, positions, norm_mix_g, w_in, b_gate, lambda_q1, lambda_k1, lambda_q2, lambda_k2,
           subln_g, w_fourier_out, w_attn_out, w_out, norm_ffn_g, w_ffn_gate, w_ffn_up,
           w_ffn_down, norm_final_g):
    batch, seq, d = x.shape
    depth = w_in.shape[0]
    t = batch * seq
    h = x.reshape(t, d)

    inv_freq = ROPE_THETA ** (-jnp.arange(0, HEAD_DIM, 2, dtype=F32) / HEAD_DIM)
    invf_full = jnp.concatenate([inv_freq, inv_freq]).reshape(1, HEAD_DIM)
    pos_col = positions.reshape(t, 1)
    qk_scale = jnp.concatenate([jnp.full((1, ATTN_WIDTH), math.log2(math.e) / math.sqrt(HEAD_DIM), F32),
                                jnp.ones((1, ATTN_WIDTH), F32)], axis=1)
    dft_consts = _dft_constants(seq)

    for l in range(depth):
        lam_init = 0.8 - 0.6 * math.exp(-0.3 * l)
        mix_cols = FOURIER_WIDTH + 3 * ATTN_WIDTH
        w_mix_b = w_in[l][:, :mix_cols].astype(BF16)
        if l == 0:
            u, cos_t, sin_t = _rmsnorm(h, norm_mix_g[l].reshape(1, d), BF16, rope=(pos_col, invf_full))
        else:
            u = _rmsnorm(h, norm_mix_g[l].reshape(1, d), BF16)
        f, _ = _in_proj(u, w_mix_b, 0, FOURIER_WIDTH, "cast")
        qk, (w_gates_b,) = _in_proj(u, w_mix_b, FOURIER_WIDTH, 2 * ATTN_WIDTH, "rope",
                                    (cos_t, sin_t, qk_scale), riders=[(w_in[l], mix_cols, 2 * d)])
        v, _ = _in_proj(u, w_mix_b, FOURIER_WIDTH + 2 * ATTN_WIDTH, ATTN_WIDTH, "cast")
        gates, (w_down_b,) = _in_proj(u, w_gates_b, 0, 2 * d, "sigmoid", (b_gate[l].reshape(1, 2 * d),),
                                      riders=[w_ffn_down[l]])

        y_f, (w_f_b, w_a_b) = _fourier_mix(
            f, batch, seq, dft_consts, [w_fourier_out[l], w_attn_out[l]])
        lam_params = jnp.stack([lambda_q1[l], lambda_k1[l], lambda_q2[l], lambda_k2[l]]).astype(F32)
        o_a, (w_gate_b, w_up_b, w_out_b) = _diff_attention(
            qk, v, lam_params, subln_g[l].reshape(1, 2 * HEAD_DIM).astype(F32), batch, seq, lam_init,
            [w_ffn_gate[l], w_ffn_up[l], w_out[l]])

        merged = _merge(y_f, o_a, w_f_b, w_a_b, gates)
        h, hg, ssq = _mm_residual_stats(merged, w_out_b, h, norm_ffn_g[l].reshape(1, d).astype(F32))
        hid = _swiglu(hg, ssq, w_gate_b, w_up_b)
        h = _mm_residual(hid, w_down_b, h)

    out = _rmsnorm(h, norm_final_g.reshape(1, d), x.dtype)
    return out.reshape(batch, seq, d)
```
